```python
import jax, jax.numpy as jnp
from jax import lax
import numpy as np

D_MODEL = 2048
BATCH = 8
SEQ = 2048
DEPTH = 1

CHUNK = 64
LEFT_CHUNKS = 8
BAND_CHUNKS = LEFT_CHUNKS + 1
BAND = BAND_CHUNKS * CHUNK
MIX_WIDTH = D_MODEL
CONV_WIDTH = MIX_WIDTH // 2
ATTN_WIDTH = MIX_WIDTH - CONV_WIDTH
HEAD_DIM = 128
N_HEADS = ATTN_WIDTH // HEAD_DIM
CONV_KERNEL = 31
MAX_REL_DIST = 128
N_REL = 2 * MAX_REL_DIST + 1
D_FF = 4 * D_MODEL
IN_COLS = 2 * CONV_WIDTH + 3 * ATTN_WIDTH
EPS = 1e-6

kernel_name = "hybrid_conformer_conv_chunked_relbias_attn_block"


def rms_norm(x, g):
    xf = x.astype(jnp.float32)
    y = xf * lax.rsqrt(jnp.mean(xf * xf, axis=-1, keepdims=True) + EPS)
    return (y * g.astype(jnp.float32)).astype(x.dtype)


def layer_norm(x, g, b):
    xf = x.astype(jnp.float32)
    mu = jnp.mean(xf, axis=-1, keepdims=True)
    xc = xf - mu
    var = jnp.mean(xc * xc, axis=-1, keepdims=True)
    y = xc * lax.rsqrt(var + EPS) * g.astype(jnp.float32) + b.astype(jnp.float32)
    return y.astype(x.dtype)


def conv_mixer(val, gate, w_dw, b_dw, ln_g, ln_b):
    h = val * jax.nn.sigmoid(gate)
    h_pad = jnp.pad(h, ((0, 0), (CONV_KERNEL - 1, 0), (0, 0)))
    y = lax.conv_general_dilated(
        h_pad, w_dw[:, None, :].astype(h.dtype), window_strides=(1,), padding="VALID",
        dimension_numbers=("NWC", "WIO", "NWC"), feature_group_count=CONV_WIDTH)
    y = y + b_dw
    y = layer_norm(y, ln_g, ln_b)
    return jax.nn.silu(y)


def chunked_relbias_attention(q, k, v, q_g, k_g, rel_table):
    b, s, _ = q.shape
    nc = s // CHUNK
    q = rms_norm(q.reshape(b, s, N_HEADS, HEAD_DIM), q_g)
    k = rms_norm(k.reshape(b, s, N_HEADS, HEAD_DIM), k_g)
    v = v.reshape(b, s, N_HEADS, HEAD_DIM)
    q_c = q.reshape(b, nc, CHUNK, N_HEADS, HEAD_DIM)
    pad = ((0, 0), (LEFT_CHUNKS, 0), (0, 0), (0, 0), (0, 0))
    k_pad = jnp.pad(k.reshape(b, nc, CHUNK, N_HEADS, HEAD_DIM), pad)
    v_pad = jnp.pad(v.reshape(b, nc, CHUNK, N_HEADS, HEAD_DIM), pad)
    k_band = jnp.concatenate([k_pad[:, j:j + nc] for j in range(BAND_CHUNKS)], axis=2)
    v_band = jnp.concatenate([v_pad[:, j:j + nc] for j in range(BAND_CHUNKS)], axis=2)

    scores = jnp.einsum("bcqhd,bckhd->bhcqk", q_c, k_band).astype(jnp.float32)
    scores = scores * (HEAD_DIM ** -0.5)

    q_local = np.arange(CHUNK)[:, None]
    k_local = np.arange(BAND)[None, :]
    rel = q_local + LEFT_CHUNKS * CHUNK - k_local
    rel_idx = np.clip(rel, -MAX_REL_DIST, MAX_REL_DIST) + MAX_REL_DIST
    bias = rel_table.astype(jnp.float32)[:, rel_idx]
    scores = scores + bias[None, :, None, :, :]

    chunk_ids = jnp.arange(nc)[:, None]
    band_chunk = jnp.arange(BAND)[None, :] // CHUNK
    valid = (chunk_ids + band_chunk - LEFT_CHUNKS) >= 0
    scores = jnp.where(valid[None, None, :, None, :], scores, jnp.finfo(jnp.float32).min)

    p = jax.nn.softmax(scores, axis=-1).astype(v.dtype)
    out = jnp.einsum("bhcqk,bckhd->bcqhd", p, v_band)
    return out.reshape(b, s, ATTN_WIDTH)


def setup_inputs(seed: int = 0) -> dict:
    key = jax.random.key(seed)
    ks = jax.random.split(key, 16)
    nrm = jax.random.normal
    f32 = jnp.float32
    return {
        "x": nrm(ks[0], (BATCH, SEQ, D_MODEL), f32),
        "ln1_g": 1.0 + 0.1 * nrm(ks[1], (DEPTH, D_MODEL), f32),
        "w_in": nrm(ks[2], (DEPTH, D_MODEL, IN_COLS), f32) * D_MODEL ** -0.5,
        "w_dw": nrm(ks[3], (DEPTH, CONV_KERNEL, CONV_WIDTH), f32) * CONV_KERNEL ** -0.5,
        "b_dw": 0.02 * nrm(ks[4], (DEPTH, CONV_WIDTH), f32),
        "conv_ln_g": 1.0 + 0.1 * nrm(ks[5], (DEPTH, CONV_WIDTH), f32),
        "conv_ln_b": 0.02 * nrm(ks[6], (DEPTH, CONV_WIDTH), f32),
        "q_norm_g": 1.0 + 0.1 * nrm(ks[7], (DEPTH, HEAD_DIM), f32),
        "k_norm_g": 1.0 + 0.1 * nrm(ks[8], (DEPTH, HEAD_DIM), f32),
        "rel_bias": 0.2 * nrm(ks[9], (DEPTH, N_HEADS, N_REL), f32),
        "out_norm_conv_g": 1.0 + 0.1 * nrm(ks[10], (DEPTH, CONV_WIDTH), f32),
        "out_norm_attn_g": 1.0 + 0.1 * nrm(ks[11], (DEPTH, ATTN_WIDTH), f32),
        "w_out": nrm(ks[12], (DEPTH, MIX_WIDTH, D_MODEL), f32) * MIX_WIDTH ** -0.5,
        "ln2_g": 1.0 + 0.1 * nrm(ks[13], (DEPTH, D_MODEL), f32),
        "w_ff1": nrm(ks[14], (DEPTH, D_MODEL, D_FF), f32) * D_MODEL ** -0.5,
        "w_ff2": nrm(ks[15], (DEPTH, D_FF, D_MODEL), f32) * D_FF ** -0.5,
    }


def reference(x, ln1_g, w_in, w_dw, b_dw, conv_ln_g, conv_ln_b, q_norm_g, k_norm_g,
              rel_bias, out_norm_conv_g, out_norm_attn_g, w_out, ln2_g, w_ff1, w_ff2):
    split_points = [CONV_WIDTH, 2 * CONV_WIDTH, 2 * CONV_WIDTH + ATTN_WIDTH,
                    2 * CONV_WIDTH + 2 * ATTN_WIDTH]
    for l in range(DEPTH):
        h = rms_norm(x, ln1_g[l])
        u = jnp.einsum("bsd,de->bse", h, w_in[l])
        c_val, c_gate, q, k, v = jnp.split(u, split_points, axis=-1)
        y_conv = conv_mixer(c_val, c_gate, w_dw[l], b_dw[l], conv_ln_g[l], conv_ln_b[l])
        y_attn = chunked_relbias_attention(q, k, v, q_norm_g[l], k_norm_g[l], rel_bias[l])
        y = jnp.concatenate([rms_norm(y_conv, out_norm_conv_g[l]),
                             rms_norm(y_attn, out_norm_attn_g[l])], axis=-1)
        x = x + jnp.einsum("bse,ed->bsd", y, w_out[l])
        h2 = rms_norm(x, ln2_g[l])
        f = jax.nn.relu(jnp.einsum("bsd,df->bsf", h2, w_ff1[l]))
        x = x + jnp.einsum("bsf,fd->bsd", f * f, w_ff2[l])
    return x
```

```python
import functools

import jax
import jax.numpy as jnp
import numpy as np
from jax import lax
from jax.experimental import pallas as pl
from jax.experimental.pallas import tpu as pltpu

F32 = jnp.float32
BF16 = jnp.bfloat16

EPS = 1e-6
CHUNK = 64
LEFT_CHUNKS = 8
HEAD_DIM = 128
CONV_KERNEL = 31
MAX_REL_DIST = 128
LANES = 128
MASK_VALUE = -1e30

Q_BLOCK = 256
KV_BLOCKS = 3
HALO = 32
MIB = 1024 * 1024


def _rms_scale(x):
    return lax.rsqrt(jnp.mean(x * x, axis=-1, keepdims=True) + EPS)


def _in_proj_kernel(x_ref, g_ref, w_ref, o_ref, h_ref):
    @pl.when(pl.program_id(1) == 0)
    def _():
        x = x_ref[...]
        h_ref[...] = (x * _rms_scale(x) * g_ref[...]).astype(BF16)

    o_ref[...] = jnp.dot(h_ref[...], w_ref[...],
                         preferred_element_type=F32).astype(o_ref.dtype)


def _in_proj(x2, g, w, tm=512, tn=1024):
    m, d = x2.shape
    n = w.shape[1]
    return pl.pallas_call(
        _in_proj_kernel,
        grid=(m // tm, n // tn),
        in_specs=[
            pl.BlockSpec((tm, d), lambda i, j: (i, 0)),
            pl.BlockSpec((1, d), lambda i, j: (0, 0)),
            pl.BlockSpec((d, tn), lambda i, j: (0, j)),
        ],
        out_specs=pl.BlockSpec((tm, tn), lambda i, j: (i, j)),
        out_shape=jax.ShapeDtypeStruct((m, n), BF16),
        scratch_shapes=[pltpu.VMEM((tm, d), BF16)],
        compiler_params=pltpu.CompilerParams(
            dimension_semantics=("arbitrary", "arbitrary"),
            vmem_limit_bytes=40 * MIB),
        name="in_proj",
    )(x2, g, w)


def _conv_kernel(val_ref, gate_ref, hval_ref, hgate_ref, w_ref, b_ref,
                 lng_ref, lnb_ref, og_ref, o_ref, hb_ref, y_ref, *, ts, rows):
    i = pl.program_id(1)
    groups = hb_ref.shape[0]
    width = groups * LANES

    halo = hval_ref[0].astype(F32) * jax.nn.sigmoid(hgate_ref[0].astype(F32))
    halo = jnp.where(i > 0, halo, 0.0)
    cur = val_ref[0].astype(F32) * jax.nn.sigmoid(gate_ref[0].astype(F32))
    for c in range(groups):
        hb_ref[c, 0:HALO, :] = halo[:, c * LANES:(c + 1) * LANES]
        hb_ref[c, HALO:HALO + ts, :] = cur[:, c * LANES:(c + 1) * LANES]

    first = HALO - (CONV_KERNEL - 1)

    def body(c, carry):
        for r in range(ts // rows):
            acc = jnp.zeros((rows, LANES), F32)
            for k in range(CONV_KERNEL):
                acc = acc + w_ref[c, k:k + 1, :] * hb_ref[c, pl.ds(first + k + r * rows, rows), :]
            y_ref[c, r * rows:(r + 1) * rows, :] = acc + b_ref[c]
        return carry

    lax.fori_loop(0, groups, body, 0)

    y = y_ref[...]
    mu = jnp.sum(jnp.sum(y, axis=0), axis=-1, keepdims=True) / width
    yc = y - mu[None]
    var = jnp.sum(jnp.sum(yc * yc, axis=0), axis=-1, keepdims=True) / width
    z = yc * lax.rsqrt(var + EPS)[None] * lng_ref[...] + lnb_ref[...]
    s = z * jax.nn.sigmoid(z)
    ms = jnp.sum(jnp.sum(s * s, axis=0), axis=-1, keepdims=True) / width
    out = s * lax.rsqrt(ms + EPS)[None] * og_ref[...]
    for c in range(groups):
        o_ref[0, :, c * LANES:(c + 1) * LANES] = out[c].astype(o_ref.dtype)


def _conv_branch(u3, w_dw, b_dw, ln_g, ln_b, out_g, ts=256, rows=64):
    b, s, _ = u3.shape
    cw = w_dw.shape[1]
    groups = cw // LANES
    w_g = w_dw.reshape(CONV_KERNEL, groups, LANES).transpose(1, 0, 2)
    as_groups = lambda v: v.reshape(groups, 1, LANES)
    per_halo = ts // HALO
    const3 = lambda bi, i: (0, 0, 0)
    return pl.pallas_call(
        functools.partial(_conv_kernel, ts=ts, rows=rows),
        grid=(b, s // ts),
        in_specs=[
            pl.BlockSpec((1, ts, cw), lambda bi, i: (bi, i, 0)),
            pl.BlockSpec((1, ts, cw), lambda bi, i: (bi, i, 1)),
            pl.BlockSpec((1, HALO, cw), lambda bi, i: (bi, jnp.maximum(i * per_halo - 1, 0), 0)),
            pl.BlockSpec((1, HALO, cw), lambda bi, i: (bi, jnp.maximum(i * per_halo - 1, 0), 1)),
            pl.BlockSpec((groups, CONV_KERNEL, LANES), const3),
            pl.BlockSpec((groups, 1, LANES), const3),
            pl.BlockSpec((groups, 1, LANES), const3),
            pl.BlockSpec((groups, 1, LANES), const3),
            pl.BlockSpec((groups, 1, LANES), const3),
        ],
        out_specs=pl.BlockSpec((1, ts, cw), lambda bi, i: (bi, i, 0)),
        out_shape=jax.ShapeDtypeStruct((b, s, cw), BF16),
        scratch_shapes=[pltpu.VMEM((groups, HALO + ts, LANES), F32),
                        pltpu.VMEM((groups, ts, LANES), F32)],
        compiler_params=pltpu.CompilerParams(
            dimension_semantics=("arbitrary", "arbitrary"),
            vmem_limit_bytes=32 * MIB),
        name="conv_branch",
    )(u3, u3, u3, u3, w_g, as_groups(b_dw), as_groups(ln_g), as_groups(ln_b), as_groups(out_g))


def _rel_bias_kernel(row_ref, mask_ref, o_ref):
    width = row_ref.shape[-1]
    x = jnp.broadcast_to(row_ref[0], (Q_BLOCK, width))
    x = pltpu.roll(x, 0, 1, stride=1, stride_axis=0)
    o_ref[0] = x[:, Q_BLOCK:] + mask_ref[...]


def _band_mask():
    r = np.arange(Q_BLOCK)[:, None] // CHUNK
    c = np.arange(KV_BLOCKS * Q_BLOCK)[None, :] // CHUNK
    visible = (c >= r) & (c <= r + LEFT_CHUNKS)
    return np.where(visible, 0.0, MASK_VALUE).astype(np.float32)


def _rel_bias(rel_table):
    h, n_rel = rel_table.shape
    kw = KV_BLOCKS * Q_BLOCK
    width = Q_BLOCK + kw
    left = kw - MAX_REL_DIST
    right = width - left - n_rel
    row = jnp.pad(rel_table[:, ::-1], ((0, 0), (left, right)), mode="edge")
    row = row.reshape(h, 1, width)
    return pl.pallas_call(
        _rel_bias_kernel,
        grid=(h,),
        in_specs=[pl.BlockSpec((1, 1, width), lambda i: (i, 0, 0)),
                  pl.BlockSpec((Q_BLOCK, kw), lambda i: (0, 0))],
        out_specs=pl.BlockSpec((1, Q_BLOCK, kw), lambda i: (i, 0, 0)),
        out_shape=jax.ShapeDtypeStruct((h, Q_BLOCK, kw), F32),
        compiler_params=pltpu.CompilerParams(dimension_semantics=("arbitrary",)),
        name="rel_bias",
    )(row, jnp.asarray(_band_mask()))


def _attn_kernel(q_ref, k0_ref, k1_ref, k2_ref, v0_ref, v1_ref, v2_ref, bias_ref,
                 qg_ref, kg_ref, og_ref, o_ref, acc_ref):
    i = pl.program_id(1)
    k_refs = (k0_ref, k1_ref, k2_ref)
    v_refs = (v0_ref, v1_ref, v2_ref)
    n_heads = acc_ref.shape[1] // HEAD_DIM
    qg = qg_ref[...] * (HEAD_DIM ** -0.5)
    kg = kg_ref[...]
    nt = (((1,), (1,)), ((), ()))
    for h in range(n_heads):
        sl = slice(h * HEAD_DIM, (h + 1) * HEAD_DIM)
        q = q_ref[0, :, sl].astype(F32)
        q = (q * _rms_scale(q) * qg).astype(BF16)
        scores = []
        for j in range(KV_BLOCKS):
            k = k_refs[j][0, :, sl].astype(F32)
            k = (k * _rms_scale(k) * kg).astype(BF16)
            s = lax.dot_general(q, k, nt, preferred_element_type=F32)
            s = s + bias_ref[h, :, j * Q_BLOCK:(j + 1) * Q_BLOCK]
            if j < KV_BLOCKS - 1:
                s = s + jnp.where(i + j >= KV_BLOCKS - 1, 0.0, MASK_VALUE)
            scores.append(s)
        m = functools.reduce(jnp.maximum, [jnp.max(s, axis=-1, keepdims=True) for s in scores])
        l = jnp.zeros_like(m)
        o = jnp.zeros((Q_BLOCK, HEAD_DIM), F32)
        for j in range(KV_BLOCKS):
            p = jnp.exp(scores[j] - m)
            l = l + jnp.sum(p, axis=-1, keepdims=True)
            o = o + jnp.dot(p.astype(BF16), v_refs[j][0, :, sl], preferred_element_type=F32)
        acc_ref[:, sl] = o / l
    y = acc_ref[...]
    o_ref[0] = (y * _rms_scale(y) * og_ref[...]).astype(o_ref.dtype)


def _attention(u3, bias, q_g, k_g, out_g, aw, col0):
    b, s, _ = u3.shape
    h = bias.shape[0]
    qcol = col0 // aw
    kv_map = lambda col, j: (lambda bi, i: (bi, jnp.maximum(i + j - (KV_BLOCKS - 1), 0), col))
    blk = (1, Q_BLOCK, aw)
    in_specs = [pl.BlockSpec(blk, lambda bi, i: (bi, i, qcol))]
    in_specs += [pl.BlockSpec(blk, kv_map(qcol + 1, j)) for j in range(KV_BLOCKS)]
    in_specs += [pl.BlockSpec(blk, kv_map(qcol + 2, j)) for j in range(KV_BLOCKS)]
    in_specs += [
        pl.BlockSpec((h, Q_BLOCK, KV_BLOCKS * Q_BLOCK), lambda bi, i: (0, 0, 0)),
        pl.BlockSpec((1, HEAD_DIM), lambda bi, i: (0, 0)),
        pl.BlockSpec((1, HEAD_DIM), lambda bi, i: (0, 0)),
        pl.BlockSpec((1, aw), lambda bi, i: (0, 0)),
    ]
    return pl.pallas_call(
        _attn_kernel,
        grid=(b, s // Q_BLOCK),
        in_specs=in_specs,
        out_specs=pl.BlockSpec(blk, lambda bi, i: (bi, i, 0)),
        out_shape=jax.ShapeDtypeStruct((b, s, aw), BF16),
        scratch_shapes=[pltpu.VMEM((Q_BLOCK, aw), F32)],
        compiler_params=pltpu.CompilerParams(
            dimension_semantics=("arbitrary", "arbitrary"),
            vmem_limit_bytes=40 * MIB),
        name="attention",
    )(u3, *([u3] * (2 * KV_BLOCKS)), bias, q_g.reshape(1, HEAD_DIM),
      k_g.reshape(1, HEAD_DIM), out_g.reshape(1, aw))


def _out_proj_kernel(x_ref, yc_ref, ya_ref, wc_ref, wa_ref, o_ref):
    acc = jnp.dot(yc_ref[...], wc_ref[...], preferred_element_type=F32)
    acc = acc + jnp.dot(ya_ref[...], wa_ref[...], preferred_element_type=F32)
    o_ref[...] = x_ref[...] + acc


def _out_proj(x2, yc2, ya2, w_out, tm=512):
    m, d = x2.shape
    cw = yc2.shape[1]
    aw = ya2.shape[1]
    return pl.pallas_call(
        _out_proj_kernel,
        grid=(m // tm,),
        in_specs=[
            pl.BlockSpec((tm, d), lambda i: (i, 0)),
            pl.BlockSpec((tm, cw), lambda i: (i, 0)),
            pl.BlockSpec((tm, aw), lambda i: (i, 0)),
            pl.BlockSpec((cw, d), lambda i: (0, 0)),
            pl.BlockSpec((aw, d), lambda i: (0, 0)),
        ],
        out_specs=pl.BlockSpec((tm, d), lambda i: (i, 0)),
        out_shape=jax.ShapeDtypeStruct((m, d), F32),
        compiler_params=pltpu.CompilerParams(
            dimension_semantics=("arbitrary",),
            vmem_limit_bytes=48 * MIB),
        name="out_proj",
    )(x2, yc2, ya2, w_out[:cw], w_out[cw:])


def _ffn_kernel(x_ref, g_ref, w1_ref, w2_ref, o_ref, h_ref):
    f = pl.program_id(1)

    @pl.when(f == 0)
    def _():
        x = x_ref[...]
        h_ref[...] = (x * _rms_scale(x) * g_ref[...]).astype(BF16)
        o_ref[...] = x

    a = jnp.dot(h_ref[...], w1_ref[...], preferred_element_type=F32)
    a = jnp.maximum(a, 0.0)
    a = (a * a).astype(BF16)
    o_ref[...] += jnp.dot(a, w2_ref[...], preferred_element_type=F32)


def _ffn(x2, g, w1, w2, tm=512, tf=1024):
    m, d = x2.shape
    ff = w1.shape[1]
    return pl.pallas_call(
        _ffn_kernel,
        grid=(m // tm, ff // tf),
        in_specs=[
            pl.BlockSpec((tm, d), lambda i, f: (i, 0)),
            pl.BlockSpec((1, d), lambda i, f: (0, 0)),
            pl.BlockSpec((d, tf), lambda i, f: (0, f)),
            pl.BlockSpec((tf, d), lambda i, f: (f, 0)),
        ],
        out_specs=pl.BlockSpec((tm, d), lambda i, f: (i, 0)),
        out_shape=jax.ShapeDtypeStruct((m, d), F32),
        scratch_shapes=[pltpu.VMEM((tm, d), BF16)],
        compiler_params=pltpu.CompilerParams(
            dimension_semantics=("arbitrary", "arbitrary"),
            vmem_limit_bytes=48 * MIB),
        name="ffn",
    )(x2, g, w1, w2)


def kernel(x, ln1_g, w_in, w_dw, b_dw, conv_ln_g, conv_ln_b, q_norm_g, k_norm_g, rel_bias,
           out_norm_conv_g, out_norm_attn_g, w_out, ln2_g, w_ff1, w_ff2):
    b, s, d = x.shape
    depth = w_in.shape[0]
    cw = w_dw.shape[2]
    aw = out_norm_attn_g.shape[1]
    x2 = x.reshape(b * s, d)
    for l in range(depth):
        u = _in_proj(x2, ln1_g[l].reshape(1, d), w_in[l].astype(BF16))
        u3 = u.reshape(b, s, u.shape[1])
        yc = _conv_branch(u3, w_dw[l], b_dw[l], conv_ln_g[l], conv_ln_b[l], out_norm_conv_g[l])
        bias = _rel_bias(rel_bias[l])
        ya = _attention(u3, bias, q_norm_g[l], k_norm_g[l], out_norm_attn_g[l], aw, 2 * cw)
        x2 = _out_proj(x2, yc.reshape(b * s, cw), ya.reshape(b * s, aw), w_out[l].astype(BF16))
        x2 = _ffn(x2, ln2_g[l].reshape(1, d), w_ff1[l].astype(BF16), w_ff2[l].astype(BF16))
    return x2.reshape(b, s, d)
```

```python
import functools
import math

import jax
import jax.numpy as jnp
import numpy as np
from jax import lax
from jax.experimental import pallas as pl
from jax.experimental.pallas import tpu as pltpu

F32 = jnp.float32
BF16 = jnp.bfloat16

EPS = 1e-6
CHUNK = 64
LEFT_CHUNKS = 8
HEAD_DIM = 128
CONV_KERNEL = 31
MAX_REL_DIST = 128
LANES = 128
MASK_VALUE = -1e30
LOG2E = math.log2(math.e)

Q_BLOCK = 256
KV_BLOCKS = 3
HALO = 32
MIB = 1024 * 1024


def _rms_scale(x):
    return lax.rsqrt(jnp.mean(x * x, axis=-1, keepdims=True) + EPS)


def _resident(shape):
    zeros = (0,) * len(shape)
    return pl.BlockSpec(shape, lambda *_: zeros, pipeline_mode=pl.Buffered(1))


def _in_proj_kernel(x_ref, g_ref, w_ref, qg_ref, kg_ref, vg_ref, q_ref, k_ref, v_ref, h_ref,
                    *, sub):
    tm = x_ref.shape[0]
    cw2 = vg_ref.shape[1]
    aw = q_ref.shape[1]
    n_sub = tm // sub
    for s in range(n_sub):
        rows = slice(s * sub, (s + 1) * sub)
        x = x_ref[rows, :]
        h_ref[rows, :] = (x * _rms_scale(x) * g_ref[...]).astype(BF16)

    def head_norm(y, gain, out_ref, rows):
        for hd in range(aw // HEAD_DIM):
            sl = slice(hd * HEAD_DIM, (hd + 1) * HEAD_DIM)
            blk = y[:, sl]
            out_ref[rows, sl] = (blk * _rms_scale(blk) * gain).astype(out_ref.dtype)

    qg = qg_ref[...] * (HEAD_DIM ** -0.5 * LOG2E)
    kg = kg_ref[...]
    for s in range(n_sub):
        rows = slice(s * sub, (s + 1) * sub)
        h = h_ref[rows, :]
        dot = lambda lo, hi: jnp.dot(h, w_ref[:, lo:hi], preferred_element_type=F32)
        vg_ref[rows, :] = dot(0, cw2).astype(vg_ref.dtype)
        head_norm(dot(cw2, cw2 + aw), qg, q_ref, rows)
        head_norm(dot(cw2 + aw, cw2 + 2 * aw), kg, k_ref, rows)
        v_ref[rows, :] = dot(cw2 + 2 * aw, cw2 + 3 * aw).astype(v_ref.dtype)


def _in_proj(x2, g, w, q_g, k_g, cw2, aw, tm=512, sub=256):
    m, d = x2.shape
    row_blk = lambda width: pl.BlockSpec((tm, width), lambda i: (i, 0))
    return pl.pallas_call(
        functools.partial(_in_proj_kernel, sub=sub),
        grid=(m // tm,),
        in_specs=[
            row_blk(d),
            _resident((1, d)),
            _resident(w.shape),
            _resident((1, HEAD_DIM)),
            _resident((1, HEAD_DIM)),
        ],
        out_specs=[row_blk(cw2), row_blk(aw), row_blk(aw), row_blk(aw)],
        out_shape=[jax.ShapeDtypeStruct((m, cw2), BF16)] + [jax.ShapeDtypeStruct((m, aw), BF16)] * 3,
        scratch_shapes=[pltpu.VMEM((tm, d), BF16)],
        compiler_params=pltpu.CompilerParams(
            dimension_semantics=("arbitrary",),
            vmem_limit_bytes=56 * MIB),
        name="in_proj",
    )(x2, g, w, q_g.reshape(1, HEAD_DIM), k_g.reshape(1, HEAD_DIM))


def _conv_kernel(val_ref, gate_ref, hval_ref, hgate_ref, w_ref, b_ref,
                 lng_ref, lnb_ref, og_ref, o_ref, hb_ref, y_ref, *, ts, rows):
    i = pl.program_id(1)
    groups = hb_ref.shape[0]
    width = groups * LANES

    halo = hval_ref[0].astype(F32) * jax.nn.sigmoid(hgate_ref[0].astype(F32))
    halo = jnp.where(i > 0, halo, 0.0)
    cur = val_ref[0].astype(F32) * jax.nn.sigmoid(gate_ref[0].astype(F32))
    for c in range(groups):
        hb_ref[c, 0:HALO, :] = halo[:, c * LANES:(c + 1) * LANES]
        hb_ref[c, HALO:HALO + ts, :] = cur[:, c * LANES:(c + 1) * LANES]

    first = HALO - (CONV_KERNEL - 1)

    def body(c, carry):
        for r in range(ts // rows):
            acc = jnp.zeros((rows, LANES), F32)
            for k in range(CONV_KERNEL):
                acc = acc + w_ref[c, k:k + 1, :] * hb_ref[c, pl.ds(first + k + r * rows, rows), :]
            y_ref[c, r * rows:(r + 1) * rows, :] = acc + b_ref[c]
        return carry

    lax.fori_loop(0, groups, body, 0)

    y = y_ref[...]
    mu = jnp.sum(jnp.sum(y, axis=0), axis=-1, keepdims=True) / width
    yc = y - mu[None]
    var = jnp.sum(jnp.sum(yc * yc, axis=0), axis=-1, keepdims=True) / width
    z = yc * lax.rsqrt(var + EPS)[None] * lng_ref[...] + lnb_ref[...]
    s = z * jax.nn.sigmoid(z)
    ms = jnp.sum(jnp.sum(s * s, axis=0), axis=-1, keepdims=True) / width
    out = s * lax.rsqrt(ms + EPS)[None] * og_ref[...]
    for c in range(groups):
        o_ref[0, :, c * LANES:(c + 1) * LANES] = out[c].astype(o_ref.dtype)


def _conv_branch(vg3, w_dw, b_dw, ln_g, ln_b, out_g, ts=256, rows=64):
    b, s, _ = vg3.shape
    cw = w_dw.shape[1]
    groups = cw // LANES
    w_g = w_dw.reshape(CONV_KERNEL, groups, LANES).transpose(1, 0, 2)
    as_groups = lambda v: v.reshape(groups, 1, LANES)
    per_halo = ts // HALO
    const3 = lambda bi, i: (0, 0, 0)
    return pl.pallas_call(
        functools.partial(_conv_kernel, ts=ts, rows=rows),
        grid=(b, s // ts),
        in_specs=[
            pl.BlockSpec((1, ts, cw), lambda bi, i: (bi, i, 0)),
            pl.BlockSpec((1, ts, cw), lambda bi, i: (bi, i, 1)),
            pl.BlockSpec((1, HALO, cw), lambda bi, i: (bi, jnp.maximum(i * per_halo - 1, 0), 0)),
            pl.BlockSpec((1, HALO, cw), lambda bi, i: (bi, jnp.maximum(i * per_halo - 1, 0), 1)),
            pl.BlockSpec((groups, CONV_KERNEL, LANES), const3),
            pl.BlockSpec((groups, 1, LANES), const3),
            pl.BlockSpec((groups, 1, LANES), const3),
            pl.BlockSpec((groups, 1, LANES), const3),
            pl.BlockSpec((groups, 1, LANES), const3),
        ],
        out_specs=pl.BlockSpec((1, ts, cw), lambda bi, i: (bi, i, 0)),
        out_shape=jax.ShapeDtypeStruct((b, s, cw), BF16),
        scratch_shapes=[pltpu.VMEM((groups, HALO + ts, LANES), F32),
                        pltpu.VMEM((groups, ts, LANES), F32)],
        compiler_params=pltpu.CompilerParams(
            dimension_semantics=("arbitrary", "arbitrary"),
            vmem_limit_bytes=32 * MIB),
        name="conv_branch",
    )(vg3, vg3, vg3, vg3, w_g, as_groups(b_dw), as_groups(ln_g), as_groups(ln_b), as_groups(out_g))


def _rel_bias_kernel(row_ref, mask_ref, o_ref):
    t = pl.program_id(0)
    width = row_ref.shape[-1]
    x = jnp.broadcast_to(row_ref[0] * LOG2E, (Q_BLOCK, width))
    x = pltpu.roll(x, 0, 1, stride=1, stride_axis=0)
    bias = x[:, Q_BLOCK:] + mask_ref[...]
    col = lax.broadcasted_iota(jnp.int32, bias.shape, 1)
    o_ref[0, 0] = jnp.where(col < (KV_BLOCKS - 1 - t) * Q_BLOCK, MASK_VALUE, bias)


def _band_mask():
    r = np.arange(Q_BLOCK)[:, None] // CHUNK
    c = np.arange(KV_BLOCKS * Q_BLOCK)[None, :] // CHUNK
    visible = (c >= r) & (c <= r + LEFT_CHUNKS)
    return np.where(visible, 0.0, MASK_VALUE).astype(np.float32)


def _rel_bias(rel_table):
    h, n_rel = rel_table.shape
    kw = KV_BLOCKS * Q_BLOCK
    width = Q_BLOCK + kw
    left = kw - MAX_REL_DIST
    right = width - left - n_rel
    row = jnp.pad(rel_table[:, ::-1], ((0, 0), (left, right)), mode="edge")
    row = row.reshape(h, 1, width)
    return pl.pallas_call(
        _rel_bias_kernel,
        grid=(KV_BLOCKS, h),
        in_specs=[pl.BlockSpec((1, 1, width), lambda t, i: (i, 0, 0)),
                  pl.BlockSpec((Q_BLOCK, kw), lambda t, i: (0, 0))],
        out_specs=pl.BlockSpec((1, 1, Q_BLOCK, kw), lambda t, i: (t, i, 0, 0)),
        out_shape=jax.ShapeDtypeStruct((KV_BLOCKS, h, Q_BLOCK, kw), F32),
        compiler_params=pltpu.CompilerParams(dimension_semantics=("arbitrary", "arbitrary")),
        name="rel_bias",
    )(row, jnp.asarray(_band_mask()))


def _attn_kernel(q_ref, k0_ref, k1_ref, k2_ref, v0_ref, v1_ref, v2_ref, bias_ref,
                 og_ref, o_ref, acc_ref, vext_ref):
    k_refs = (k0_ref, k1_ref, k2_ref)
    v_refs = (v0_ref, v1_ref, v2_ref)
    n_heads = acc_ref.shape[1] // HEAD_DIM
    nt = (((1,), (1,)), ((), ()))

    @pl.when((pl.program_id(0) == 0) & (pl.program_id(1) == 0))
    def _():
        vext_ref[:, :, :, HEAD_DIM:] = jnp.ones(vext_ref.shape[:3] + (HEAD_DIM,), BF16)

    for h in range(n_heads):
        sl = slice(h * HEAD_DIM, (h + 1) * HEAD_DIM)
        q = q_ref[0, :, sl]
        scores = []
        for j in range(KV_BLOCKS):
            s = lax.dot_general(q, k_refs[j][0, :, sl], nt, preferred_element_type=F32)
            scores.append(s + bias_ref[0, h, :, j * Q_BLOCK:(j + 1) * Q_BLOCK])
            vext_ref[j, h, :, :HEAD_DIM] = v_refs[j][0, :, sl]
        m = functools.reduce(jnp.maximum, [jnp.max(s, axis=-1, keepdims=True) for s in scores])
        o = jnp.zeros((Q_BLOCK, 2 * HEAD_DIM), F32)
        for j in range(KV_BLOCKS):
            p = jnp.exp2(scores[j] - m).astype(BF16)
            o = o + jnp.dot(p, vext_ref[j, h], preferred_element_type=F32)
        acc_ref[:, sl] = o[:, :HEAD_DIM] / o[:, HEAD_DIM:]
    y = acc_ref[...]
    o_ref[0] = (y * _rms_scale(y) * og_ref[...]).astype(o_ref.dtype)


def _attention(q3, k3, v3, bias, out_g):
    b, s, aw = q3.shape
    h = bias.shape[1]
    kv_map = lambda j: (lambda bi, i: (bi, jnp.maximum(i + j - (KV_BLOCKS - 1), 0), 0))
    blk = (1, Q_BLOCK, aw)
    in_specs = [pl.BlockSpec(blk, lambda bi, i: (bi, i, 0))]
    in_specs += [pl.BlockSpec(blk, kv_map(j)) for j in range(KV_BLOCKS)]
    in_specs += [pl.BlockSpec(blk, kv_map(j)) for j in range(KV_BLOCKS)]
    in_specs += [
        pl.BlockSpec((1, h, Q_BLOCK, KV_BLOCKS * Q_BLOCK),
                     lambda bi, i: (jnp.minimum(i, KV_BLOCKS - 1), 0, 0, 0)),
        pl.BlockSpec((1, aw), lambda bi, i: (0, 0)),
    ]
    return pl.pallas_call(
        _attn_kernel,
        grid=(b, s // Q_BLOCK),
        in_specs=in_specs,
        out_specs=pl.BlockSpec(blk, lambda bi, i: (bi, i, 0)),
        out_shape=jax.ShapeDtypeStruct((b, s, aw), BF16),
        scratch_shapes=[pltpu.VMEM((Q_BLOCK, aw), F32),
                        pltpu.VMEM((KV_BLOCKS, h, Q_BLOCK, 2 * HEAD_DIM), BF16)],
        compiler_params=pltpu.CompilerParams(
            dimension_semantics=("arbitrary", "arbitrary"),
            vmem_limit_bytes=40 * MIB),
        name="attention",
    )(q3, k3, k3, k3, v3, v3, v3, bias, out_g.reshape(1, aw))


def _out_proj_kernel(x_ref, yc_ref, ya_ref, wc_ref, wa_ref, o_ref):
    acc = jnp.dot(yc_ref[...], wc_ref[...], preferred_element_type=F32)
    acc = acc + jnp.dot(ya_ref[...], wa_ref[...], preferred_element_type=F32)
    o_ref[...] = x_ref[...] + acc


def _out_proj(x2, yc2, ya2, w_out, tm=512):
    m, d = x2.shape
    cw = yc2.shape[1]
    aw = ya2.shape[1]
    return pl.pallas_call(
        _out_proj_kernel,
        grid=(m // tm,),
        in_specs=[
            pl.BlockSpec((tm, d), lambda i: (i, 0)),
            pl.BlockSpec((tm, cw), lambda i: (i, 0)),
            pl.BlockSpec((tm, aw), lambda i: (i, 0)),
            pl.BlockSpec((cw, d), lambda i: (0, 0)),
            pl.BlockSpec((aw, d), lambda i: (0, 0)),
        ],
        out_specs=pl.BlockSpec((tm, d), lambda i: (i, 0)),
        out_shape=jax.ShapeDtypeStruct((m, d), F32),
        compiler_params=pltpu.CompilerParams(
            dimension_semantics=("arbitrary",),
            vmem_limit_bytes=48 * MIB),
        name="out_proj",
    )(x2, yc2, ya2, w_out[:cw], w_out[cw:])


def _ffn_kernel(x_ref, g_ref, w1_ref, w2_ref, o_ref, h_ref):
    f = pl.program_id(1)

    @pl.when(f == 0)
    def _():
        x = x_ref[...]
        h_ref[...] = (x * _rms_scale(x) * g_ref[...]).astype(BF16)
        o_ref[...] = x

    a = jnp.dot(h_ref[...], w1_ref[...], preferred_element_type=F32)
    a = jnp.maximum(a, 0.0)
    a = (a * a).astype(BF16)
    o_ref[...] += jnp.dot(a, w2_ref[...], preferred_element_type=F32)


def _ffn(x2, g, w1, w2, tm=512, tf=1024):
    m, d = x2.shape
    ff = w1.shape[1]
    return pl.pallas_call(
        _ffn_kernel,
        grid=(m // tm, ff // tf),
        in_specs=[
            pl.BlockSpec((tm, d), lambda i, f: (i, 0)),
            pl.BlockSpec((1, d), lambda i, f: (0, 0)),
            pl.BlockSpec((d, tf), lambda i, f: (0, f)),
            pl.BlockSpec((tf, d), lambda i, f: (f, 0)),
        ],
        out_specs=pl.BlockSpec((tm, d), lambda i, f: (i, 0)),
        out_shape=jax.ShapeDtypeStruct((m, d), F32),
        scratch_shapes=[pltpu.VMEM((tm, d), BF16)],
        compiler_params=pltpu.CompilerParams(
            dimension_semantics=("arbitrary", "arbitrary"),
            vmem_limit_bytes=48 * MIB),
        name="ffn",
    )(x2, g, w1, w2)


def kernel(x, ln1_g, w_in, w_dw, b_dw, conv_ln_g, conv_ln_b, q_norm_g, k_norm_g, rel_bias,
           out_norm_conv_g, out_norm_attn_g, w_out, ln2_g, w_ff1, w_ff2):
    b, s, d = x.shape
    depth = w_in.shape[0]
    cw = w_dw.shape[2]
    aw = out_norm_attn_g.shape[1]
    x2 = x.reshape(b * s, d)
    for l in range(depth):
        vg, q, k, v = _in_proj(x2, ln1_g[l].reshape(1, d), w_in[l].astype(BF16),
                               q_norm_g[l], k_norm_g[l], 2 * cw, aw)
        yc = _conv_branch(vg.reshape(b, s, 2 * cw), w_dw[l], b_dw[l], conv_ln_g[l],
                          conv_ln_b[l], out_norm_conv_g[l])
        bias = _rel_bias(rel_bias[l])
        ya = _attention(q.reshape(b, s, aw), k.reshape(b, s, aw), v.reshape(b, s, aw),
                        bias, out_norm_attn_g[l])
        x2 = _out_proj(x2, yc.reshape(b * s, cw), ya.reshape(b * s, aw), w_out[l].astype(BF16))
        x2 = _ffn(x2, ln2_g[l].reshape(1, d), w_ff1[l].astype(BF16), w_ff2[l].astype(BF16))
    return x2.reshape(b, s, d)
```

```python
import functools
import math

import jax
import jax.numpy as jnp
import numpy as np
from jax import lax
from jax.experimental import pallas as pl
from jax.experimental.pallas import tpu as pltpu

F32 = jnp.float32
BF16 = jnp.bfloat16

EPS = 1e-6
CHUNK = 64
LEFT_CHUNKS = 8
HEAD_DIM = 128
CONV_KERNEL = 31
MAX_REL_DIST = 128
LANES = 128
MASK_VALUE = -1e30
LOG2E = math.log2(math.e)

Q_BLOCK = 256
KV_BLOCKS = 3
HALO = 32
ROW_TILE = 512
SUB_ROWS = 256
CONV_ROWS = 64
NORM_ROWS = 64
MXU_COLS = 512
QKV_COLS = 256
MIB = 1024 * 1024


def _rms_scale(x):
    return lax.rsqrt(jnp.mean(x * x, axis=-1, keepdims=True) + EPS)


def _resident(shape, index=None):
    index = index or (0,) * len(shape)
    return pl.BlockSpec(shape, lambda *_: index, pipeline_mode=pl.Buffered(1))


def _vg_proj_kernel(x_ref, g_ref, w_ref, *refs, n_cast):
    cast_in = refs[:n_cast]
    vg_ref, h_ref = refs[n_cast:n_cast + 2]
    cast_out = refs[n_cast + 2:]
    tm = x_ref.shape[0]
    n = vg_ref.shape[1]

    def norm_rows(r0, nrows):
        x = x_ref[r0:r0 + nrows, :]
        h_ref[r0:r0 + nrows, :] = (x * _rms_scale(x) * g_ref[...]).astype(BF16)

    def cast(src, dst):
        dst[...] = src[...].astype(dst.dtype)

    norm_rows(0, SUB_ROWS)
    side = [functools.partial(norm_rows, r0, NORM_ROWS)
            for r0 in range(SUB_ROWS, tm, NORM_ROWS)]
    side += [functools.partial(cast, src, dst) for src, dst in zip(cast_in, cast_out)]
    chunks = [(s, c0) for s in range(tm // SUB_ROWS) for c0 in range(0, n, MXU_COLS)]
    first_sub = n // MXU_COLS
    for idx, (s, c0) in enumerate(chunks):
        rows = slice(s * SUB_ROWS, (s + 1) * SUB_ROWS)
        vg_ref[rows, c0:c0 + MXU_COLS] = jnp.dot(
            h_ref[rows, :], w_ref[:, c0:c0 + MXU_COLS],
            preferred_element_type=F32).astype(vg_ref.dtype)
        n_norm = (tm - SUB_ROWS) // NORM_ROWS
        take = -(-n_norm // first_sub) if idx < first_sub else -(-len(side) // (len(chunks) - idx))
        for _ in range(min(take, len(side))):
            side.pop(0)()
    while side:
        side.pop(0)()


def _vg_proj(x2, g, w_vg, casts):
    m, d = x2.shape
    n = w_vg.shape[1]
    steps = m // ROW_TILE
    row_blk = lambda width: pl.BlockSpec((ROW_TILE, width), lambda i: (i, 0))
    cast_specs, cast_shapes, cast_args = [], [], []
    for arr, width, col in casts:
        slab = arr.shape[0] // steps
        cast_specs.append(pl.BlockSpec((slab, width), lambda i, col=col: (i, col)))
        cast_shapes.append(jax.ShapeDtypeStruct((arr.shape[0], width), BF16))
        cast_args.append(arr)
    out_cast_specs = [pl.BlockSpec(spec.block_shape, lambda i: (i, 0)) for spec in cast_specs]
    outs = pl.pallas_call(
        functools.partial(_vg_proj_kernel, n_cast=len(casts)),
        grid=(steps,),
        in_specs=[row_blk(d), _resident((1, d)), _resident(w_vg.shape)] + cast_specs,
        out_specs=[row_blk(n), row_blk(d)] + out_cast_specs,
        out_shape=[jax.ShapeDtypeStruct((m, n), BF16), jax.ShapeDtypeStruct((m, d), BF16)]
        + cast_shapes,
        compiler_params=pltpu.CompilerParams(
            dimension_semantics=("arbitrary",),
            vmem_limit_bytes=56 * MIB),
        name="vg_proj",
    )(x2, g, w_vg, *cast_args)
    return outs[0], outs[1], outs[2:]


def _qkv_conv_kernel(h_ref, wq_ref, wk_ref, wv_ref, qg_ref, kg_ref,
                     val_ref, gate_ref, hval_ref, hgate_ref, cw_ref, cb_ref,
                     lng_ref, lnb_ref, og_ref,
                     q_ref, k_ref, v_ref, yc_ref, hb_ref, y_ref, *, tiles_per_seq):
    tm = h_ref.shape[0]
    aw = q_ref.shape[1]
    groups = hb_ref.shape[0]
    width = groups * LANES

    first_tile = pl.program_id(0) % tiles_per_seq == 0
    halo = hval_ref[...].astype(F32) * jax.nn.sigmoid(hgate_ref[...].astype(F32))
    halo = jnp.where(first_tile, 0.0, halo)
    for c in range(groups):
        hb_ref[c, 0:HALO, :] = halo[:, c * LANES:(c + 1) * LANES]

    def glu_rows(r0):
        cur = (val_ref[r0:r0 + CONV_ROWS, :].astype(F32)
               * jax.nn.sigmoid(gate_ref[r0:r0 + CONV_ROWS, :].astype(F32)))
        for c in range(groups):
            hb_ref[c, HALO + r0:HALO + r0 + CONV_ROWS, :] = cur[:, c * LANES:(c + 1) * LANES]

    first = HALO - (CONV_KERNEL - 1)

    anchor = [jnp.zeros((CONV_ROWS, LANES), F32)]

    def conv_rows(c, r):
        acc = anchor[0]
        for t in range(CONV_KERNEL):
            acc = acc + cw_ref[c, t:t + 1, :] * hb_ref[c, first + t + r:first + t + r + CONV_ROWS, :]
        y_ref[c, r:r + CONV_ROWS, :] = acc + cb_ref[c]

    def conv_norms(r):
        rows = slice(r, r + CONV_ROWS)
        y = y_ref[:, rows, :]
        mu = jnp.sum(jnp.sum(y, axis=0), axis=-1, keepdims=True) / width
        yc = y - mu[None]
        var = jnp.sum(jnp.sum(yc * yc, axis=0), axis=-1, keepdims=True) / width
        z = yc * lax.rsqrt(var + EPS)[None] * lng_ref[...] + lnb_ref[...]
        sw = z * jax.nn.sigmoid(z)
        ms = jnp.sum(jnp.sum(sw * sw, axis=0), axis=-1, keepdims=True) / width
        out = sw * lax.rsqrt(ms + EPS)[None] * og_ref[...]
        for c in range(groups):
            yc_ref[rows, c * LANES:(c + 1) * LANES] = out[c].astype(yc_ref.dtype)

    def head_norm(y, gain, out_ref, rows, c0):
        for hd in range(y.shape[1] // HEAD_DIM):
            blk = y[:, hd * HEAD_DIM:(hd + 1) * HEAD_DIM]
            cols = slice(c0 + hd * HEAD_DIM, c0 + (hd + 1) * HEAD_DIM)
            out_ref[rows, cols] = (blk * _rms_scale(blk) * gain).astype(out_ref.dtype)

    qg = qg_ref[...] * (HEAD_DIM ** -0.5 * LOG2E)
    kg = kg_ref[...]
    side = []
    for r in range(0, tm, CONV_ROWS):
        side.append(functools.partial(glu_rows, r))
        side += [functools.partial(conv_rows, c, r) for c in range(groups)]
        side.append(functools.partial(conv_norms, r))

    def proj_chunk(s, w_ref, gain, out_ref, c0):
        rows = slice(s * SUB_ROWS, (s + 1) * SUB_ROWS)
        y = jnp.dot(h_ref[rows, :], w_ref[:, c0:c0 + QKV_COLS], preferred_element_type=F32)
        if gain is None:
            out_ref[rows, c0:c0 + QKV_COLS] = y.astype(out_ref.dtype)
        else:
            head_norm(y, gain, out_ref, rows, c0)
        bits = lax.bitcast_convert_type(y[:CONV_ROWS, :LANES], jnp.uint32)
        bits = lax.shift_right_logical(lax.shift_right_logical(bits, jnp.uint32(16)), jnp.uint32(16))
        anchor[0] = lax.bitcast_convert_type(bits, F32)

    chunks =[functools.partial(proj_chunk, s, w_ref, gain, out_ref, c0)
              for s in range(tm // SUB_ROWS)
              for w_ref, gain, out_ref in ((wq_ref, qg, q_ref), (wk_ref, kg, k_ref), (wv_ref, None, v_ref))
              for c0 in range(0, aw, QKV_COLS)]
    side.pop(0)()
    for idx, chunk in enumerate(chunks):
        chunk()
        for _ in range(-(-len(side) // (len(chunks) - idx))):
            side.pop(0)()


def _qkv_conv(h2, wq, wk, wv, q_g, k_g, vg2, w_dw, b_dw, ln_g, ln_b, out_g, seq_len):
    m, d = h2.shape
    aw = wq.shape[1]
    cw = w_dw.shape[1]
    groups = cw // LANES
    tm = ROW_TILE
    w_g = w_dw.reshape(CONV_KERNEL, groups, LANES).transpose(1, 0, 2)
    as_groups = lambda v: v.reshape(groups, 1, LANES)
    per_halo = tm // HALO
    row_blk = lambda width, col=0: pl.BlockSpec((tm, width), lambda i: (i, col))
    halo_blk = lambda col: pl.BlockSpec(
        (HALO, cw), lambda i: (jnp.maximum(i * per_halo - 1, 0), col))
    return pl.pallas_call(
        functools.partial(_qkv_conv_kernel, tiles_per_seq=seq_len // tm),
        grid=(m // tm,),
        in_specs=[
            row_blk(d),
            _resident(wq.shape), _resident(wk.shape), _resident(wv.shape),
            _resident((1, HEAD_DIM)), _resident((1, HEAD_DIM)),
            row_blk(cw, 0), row_blk(cw, 1), halo_blk(0), halo_blk(1),
            _resident((groups, CONV_KERNEL, LANES)),
            _resident((groups, 1, LANES)), _resident((groups, 1, LANES)),
            _resident((groups, 1, LANES)), _resident((groups, 1, LANES)),
        ],
        out_specs=[row_blk(aw), row_blk(aw), row_blk(aw), row_blk(cw)],
        out_shape=[jax.ShapeDtypeStruct((m, aw), BF16)] * 3 + [jax.ShapeDtypeStruct((m, cw), BF16)],
        scratch_shapes=[pltpu.VMEM((groups, HALO + tm, LANES), F32),
                        pltpu.VMEM((groups, tm, LANES), F32)],
        compiler_params=pltpu.CompilerParams(
            dimension_semantics=("arbitrary",),
            vmem_limit_bytes=56 * MIB),
        name="qkv_conv",
    )(h2, wq, wk, wv, q_g.reshape(1, HEAD_DIM), k_g.reshape(1, HEAD_DIM),
      vg2, vg2, vg2, vg2, w_g, as_groups(b_dw), as_groups(ln_g), as_groups(ln_b), as_groups(out_g))


def _rel_bias_kernel(row_ref, mask_ref, o_ref):
    width = row_ref.shape[-1]
    x = jnp.broadcast_to(row_ref[0] * LOG2E, (Q_BLOCK, width))
    x = pltpu.roll(x, 0, 1, stride=1, stride_axis=0)
    bias = x[:, Q_BLOCK:] + mask_ref[...]
    col = lax.broadcasted_iota(jnp.int32, bias.shape, 1)
    for t in range(KV_BLOCKS):
        o_ref[t, 0] = jnp.where(col < (KV_BLOCKS - 1 - t) * Q_BLOCK, MASK_VALUE, bias)


def _band_mask():
    r = np.arange(Q_BLOCK)[:, None] // CHUNK
    c = np.arange(KV_BLOCKS * Q_BLOCK)[None, :] // CHUNK
    visible = (c >= r) & (c <= r + LEFT_CHUNKS)
    return np.where(visible, 0.0, MASK_VALUE).astype(np.float32)


def _rel_bias(rel_table):
    h, n_rel = rel_table.shape
    kw = KV_BLOCKS * Q_BLOCK
    width = Q_BLOCK + kw
    left = kw - MAX_REL_DIST
    right = width - left - n_rel
    row = jnp.pad(rel_table[:, ::-1], ((0, 0), (left, right)), mode="edge")
    row = row.reshape(h, 1, width)
    return pl.pallas_call(
        _rel_bias_kernel,
        grid=(h,),
        in_specs=[pl.BlockSpec((1, 1, width), lambda i: (i, 0, 0)),
                  pl.BlockSpec((Q_BLOCK, kw), lambda i: (0, 0))],
        out_specs=pl.BlockSpec((KV_BLOCKS, 1, Q_BLOCK, kw), lambda i: (0, i, 0, 0)),
        out_shape=jax.ShapeDtypeStruct((KV_BLOCKS, h, Q_BLOCK, kw), F32),
        compiler_params=pltpu.CompilerParams(dimension_semantics=("arbitrary",)),
        name="rel_bias",
    )(row, jnp.asarray(_band_mask()))


def _attn_kernel(q_ref, k0_ref, k1_ref, k2_ref, v0_ref, v1_ref, v2_ref, bias_ref,
                 og_ref, o_ref, acc_ref, vext_ref):
    k_refs = (k0_ref, k1_ref, k2_ref)
    v_refs = (v0_ref, v1_ref, v2_ref)
    n_heads = acc_ref.shape[1] // HEAD_DIM
    nt = (((1,), (1,)), ((), ()))

    @pl.when((pl.program_id(0) == 0) & (pl.program_id(1) == 0))
    def _():
        vext_ref[:, :, :, HEAD_DIM:] = jnp.ones(vext_ref.shape[:3] + (HEAD_DIM,), BF16)

    for h in range(n_heads):
        sl = slice(h * HEAD_DIM, (h + 1) * HEAD_DIM)
        q = q_ref[0, :, sl]
        scores = []
        for j in range(KV_BLOCKS):
            s = lax.dot_general(q, k_refs[j][0, :, sl], nt, preferred_element_type=F32)
            scores.append(s + bias_ref[0, h, :, j * Q_BLOCK:(j + 1) * Q_BLOCK])
            vext_ref[j, h, :, :HEAD_DIM] = v_refs[j][0, :, sl]
        m = functools.reduce(jnp.maximum, [jnp.max(s, axis=-1, keepdims=True) for s in scores])
        o = jnp.zeros((Q_BLOCK, 2 * HEAD_DIM), F32)
        for j in range(KV_BLOCKS):
            p = jnp.exp2(scores[j] - m).astype(BF16)
            o = o + jnp.dot(p, vext_ref[j, h], preferred_element_type=F32)
        acc_ref[:, sl] = o[:, :HEAD_DIM] / o[:, HEAD_DIM:]
    y = acc_ref[...]
    o_ref[0] = (y * _rms_scale(y) * og_ref[...]).astype(o_ref.dtype)


def _attention(q3, k3, v3, bias, out_g):
    b, s, aw = q3.shape
    h = bias.shape[1]
    kv_map = lambda j: (lambda bi, i: (bi, jnp.maximum(i + j - (KV_BLOCKS - 1), 0), 0))
    blk = (1, Q_BLOCK, aw)
    in_specs = [pl.BlockSpec(blk, lambda bi, i: (bi, i, 0))]
    in_specs += [pl.BlockSpec(blk, kv_map(j)) for j in range(KV_BLOCKS)]
    in_specs += [pl.BlockSpec(blk, kv_map(j)) for j in range(KV_BLOCKS)]
    in_specs += [
        pl.BlockSpec((1, h, Q_BLOCK, KV_BLOCKS * Q_BLOCK),
                     lambda bi, i: (jnp.minimum(i, KV_BLOCKS - 1), 0, 0, 0)),
        pl.BlockSpec((1, aw), lambda bi, i: (0, 0)),
    ]
    return pl.pallas_call(
        _attn_kernel,
        grid=(b, s // Q_BLOCK),
        in_specs=in_specs,
        out_specs=pl.BlockSpec(blk, lambda bi, i: (bi, i, 0)),
        out_shape=jax.ShapeDtypeStruct((b, s, aw), BF16),
        scratch_shapes=[pltpu.VMEM((Q_BLOCK, aw), F32),
                        pltpu.VMEM((KV_BLOCKS, h, Q_BLOCK, 2 * HEAD_DIM), BF16)],
        compiler_params=pltpu.CompilerParams(
            dimension_semantics=("arbitrary", "arbitrary"),
            vmem_limit_bytes=40 * MIB),
        name="attention",
    )(q3, k3, k3, k3, v3, v3, v3, bias, out_g.reshape(1, aw))


def _out_proj_kernel(x_ref, yc_ref, ya_ref, wc_ref, wa_ref, o_ref):
    acc = jnp.dot(yc_ref[...], wc_ref[...], preferred_element_type=F32)
    acc = acc + jnp.dot(ya_ref[...], wa_ref[...], preferred_element_type=F32)
    o_ref[...] = x_ref[...] + acc


def _out_proj(x2, yc2, ya2, w_out):
    m, d = x2.shape
    cw = yc2.shape[1]
    aw = ya2.shape[1]
    tm = ROW_TILE
    return pl.pallas_call(
        _out_proj_kernel,
        grid=(m // tm,),
        in_specs=[
            pl.BlockSpec((tm, d), lambda i: (i, 0)),
            pl.BlockSpec((tm, cw), lambda i: (i, 0)),
            pl.BlockSpec((tm, aw), lambda i: (i, 0)),
            _resident((cw, d), (0, 0)),
            _resident((aw, d), (cw // aw, 0)),
        ],
        out_specs=pl.BlockSpec((tm, d), lambda i: (i, 0)),
        out_shape=jax.ShapeDtypeStruct((m, d), F32),
        compiler_params=pltpu.CompilerParams(
            dimension_semantics=("arbitrary",),
            vmem_limit_bytes=48 * MIB),
        name="out_proj",
    )(x2, yc2, ya2, w_out, w_out)


def _ffn_kernel(x_ref, g_ref, w1_ref, w2_ref, o_ref, h_ref):
    f = pl.program_id(1)

    @pl.when(f == 0)
    def _():
        x = x_ref[...]
        h_ref[...] = (x * _rms_scale(x) * g_ref[...]).astype(BF16)
        o_ref[...] = x

    a = jnp.dot(h_ref[...], w1_ref[...], preferred_element_type=F32)
    a = jnp.maximum(a, 0.0)
    a = (a * a).astype(BF16)
    o_ref[...] += jnp.dot(a, w2_ref[...], preferred_element_type=F32)


def _ffn(x2, g, w1, w2, tm=512, tf=1024):
    m, d = x2.shape
    ff = w1.shape[1]
    return pl.pallas_call(
        _ffn_kernel,
        grid=(m // tm, ff // tf),
        in_specs=[
            pl.BlockSpec((tm, d), lambda i, f: (i, 0)),
            pl.BlockSpec((1, d), lambda i, f: (0, 0)),
            pl.BlockSpec((d, tf), lambda i, f: (0, f)),
            pl.BlockSpec((tf, d), lambda i, f: (f, 0)),
        ],
        out_specs=pl.BlockSpec((tm, d), lambda i, f: (i, 0)),
        out_shape=jax.ShapeDtypeStruct((m, d), F32),
        scratch_shapes=[pltpu.VMEM((tm, d), BF16)],
        compiler_params=pltpu.CompilerParams(
            dimension_semantics=("arbitrary", "arbitrary"),
            vmem_limit_bytes=48 * MIB),
        name="ffn",
    )(x2, g, w1, w2)


def kernel(x, ln1_g, w_in, w_dw, b_dw, conv_ln_g, conv_ln_b, q_norm_g, k_norm_g, rel_bias,
           out_norm_conv_g, out_norm_attn_g, w_out, ln2_g, w_ff1, w_ff2):
    b, s, d = x.shape
    depth = w_in.shape[0]
    cw = w_dw.shape[2]
    aw = out_norm_attn_g.shape[1]
    ff = w_ff1.shape[2]
    qcol = 2 * cw // aw
    x2 = x.reshape(b * s, d)
    for l in range(depth):
        w_vg = w_in[l, :, :2 * cw].astype(BF16)
        casts = [(w_in[l], aw, qcol), (w_in[l], aw, qcol + 1), (w_in[l], aw, qcol + 2),
                 (w_out[l], d, 0), (w_ff1[l], ff, 0), (w_ff2[l], d, 0)]
        vg, h, (wq, wk, wv, wo, w1, w2) = _vg_proj(x2, ln1_g[l].reshape(1, d), w_vg, casts)
        q, k, v, yc = _qkv_conv(h, wq, wk, wv, q_norm_g[l], k_norm_g[l], vg, w_dw[l], b_dw[l],
                                conv_ln_g[l], conv_ln_b[l], out_norm_conv_g[l], s)
        bias = _rel_bias(rel_bias[l])
        ya = _attention(q.reshape(b, s, aw), k.reshape(b, s, aw), v.reshape(b, s, aw),
                        bias, out_norm_attn_g[l])
        x2 = _out_proj(x2, yc, ya.reshape(b * s, aw), wo)
        x2 = _ffn(x2, ln2_g[l].reshape(1, d), w1, w2)
    return x2.reshape(b, s, d)
```

```python
import functools
import math

import jax
import jax.numpy as jnp
import numpy as np
from jax import lax
from jax.experimental import pallas as pl
from jax.experimental.pallas import tpu as pltpu

F32 = jnp.float32
BF16 = jnp.bfloat16

EPS = 1e-6
CHUNK = 64
LEFT_CHUNKS = 8
HEAD_DIM = 128
CONV_KERNEL = 31
MAX_REL_DIST = 128
LANES = 128
BF16_ROWS = 16
MASK_VALUE = -1e30
LOG2E = math.log2(math.e)

Q_BLOCK = 256
KV_BLOCKS = 3
HALO = 32
ROW_TILE = 512
SUB_ROWS = 256
CONV_ROWS = 64
NORM_ROWS = 64
MXU_COLS = 512
MIB = 1024 * 1024


def _rms_scale(x):
    return lax.rsqrt(jnp.mean(x * x, axis=-1, keepdims=True) + EPS)


def _resident(shape, index=None):
    index = index or (0,) * len(shape)
    return pl.BlockSpec(shape, lambda *_: index, pipeline_mode=pl.Buffered(1))


def _cast_specs(casts, steps):
    in_specs, out_specs, shapes, args = [], [], [], []
    for arr, width, col in casts:
        slab = arr.shape[0] // steps
        in_specs.append(pl.BlockSpec((slab, width), lambda i, col=col: (i, col)))
        out_specs.append(pl.BlockSpec((slab, width), lambda i: (i, 0)))
        shapes.append(jax.ShapeDtypeStruct((arr.shape[0], width), BF16))
        args.append(arr)
    return in_specs, out_specs, shapes, args


def _vg_proj_kernel(x_ref, g_ref, w_ref, *refs, n_cast):
    cast_in = refs[:n_cast]
    vg_ref, h_ref = refs[n_cast:n_cast + 2]
    cast_out = refs[n_cast + 2:]
    tm = x_ref.shape[0]
    n = w_ref.shape[1]

    def norm_rows(r0, nrows):
        x = x_ref[r0:r0 + nrows, :]
        h_ref[r0:r0 + nrows, :] = (x * _rms_scale(x) * g_ref[...]).astype(BF16)

    for r0 in range(0, tm, NORM_ROWS):
        norm_rows(r0, NORM_ROWS)
    for s in range(tm // SUB_ROWS):
        rows = slice(s * SUB_ROWS, (s + 1) * SUB_ROWS)
        for c0 in range(0, n, MXU_COLS):
            y = jnp.dot(h_ref[rows, :], w_ref[:, c0:c0 + MXU_COLS], preferred_element_type=F32)
            for g in range(MXU_COLS // LANES):
                vg_ref[c0 // LANES + g, rows, :] = y[:, g * LANES:(g + 1) * LANES].astype(vg_ref.dtype)
    for src, dst in zip(cast_in, cast_out):
        dst[...] = src[...].astype(dst.dtype)


def _vg_proj(x2, g, w_vg, casts):
    m, d = x2.shape
    n = w_vg.shape[1]
    steps = m // ROW_TILE
    cast_in, cast_out, cast_shapes, cast_args = _cast_specs(casts, steps)
    outs = pl.pallas_call(
        functools.partial(_vg_proj_kernel, n_cast=len(casts)),
        grid=(steps,),
        in_specs=[pl.BlockSpec((ROW_TILE, d), lambda i: (i, 0)),
                  _resident((1, d)), _resident(w_vg.shape)] + cast_in,
        out_specs=[pl.BlockSpec((n // LANES, ROW_TILE, LANES), lambda i: (0, i, 0)),
                   pl.BlockSpec((ROW_TILE, d), lambda i: (i, 0))] + cast_out,
        out_shape=[jax.ShapeDtypeStruct((n // LANES, m, LANES), BF16),
                   jax.ShapeDtypeStruct((m, d), BF16)] + cast_shapes,
        compiler_params=pltpu.CompilerParams(
            dimension_semantics=("arbitrary",),
            vmem_limit_bytes=48 * MIB),
        name="vg_proj",
    )(x2, g, w_vg, *cast_args)
    return outs[0], outs[1], outs[2:]


def _qkv_proj_kernel(h_ref, wq_ref, wk_ref, wv_ref, qg_ref, kg_ref, *refs, n_cast):
    cast_in = refs[:n_cast]
    q_ref, k_ref, v_ref = refs[n_cast:n_cast + 3]
    cast_out = refs[n_cast + 3:]
    tm = h_ref.shape[0]
    aw = q_ref.shape[1]

    def head_norm(y, gain, out_ref, rows):
        for hd in range(aw // HEAD_DIM):
            sl = slice(hd * HEAD_DIM, (hd + 1) * HEAD_DIM)
            blk = y[:, sl]
            out_ref[rows, sl] = (blk * _rms_scale(blk) * gain).astype(out_ref.dtype)

    qg = qg_ref[...] * (HEAD_DIM ** -0.5 * LOG2E)
    kg = kg_ref[...]
    for s in range(tm // SUB_ROWS):
        rows = slice(s * SUB_ROWS, (s + 1) * SUB_ROWS)
        h = h_ref[rows, :]
        head_norm(jnp.dot(h, wq_ref[...], preferred_element_type=F32), qg, q_ref, rows)
        head_norm(jnp.dot(h, wk_ref[...], preferred_element_type=F32), kg, k_ref, rows)
        v_ref[rows, :] = jnp.dot(h, wv_ref[...], preferred_element_type=F32).astype(v_ref.dtype)
    for src, dst in zip(cast_in, cast_out):
        dst[...] = src[...].astype(dst.dtype)


def _qkv_proj(h2, wq, wk, wv, q_g, k_g, casts):
    m, d = h2.shape
    aw = wq.shape[1]
    steps = m // ROW_TILE
    row_blk = lambda width: pl.BlockSpec((ROW_TILE, width), lambda i: (i, 0))
    cast_in, cast_out, cast_shapes, cast_args = _cast_specs(casts, steps)
    outs = pl.pallas_call(
        functools.partial(_qkv_proj_kernel, n_cast=len(casts)),
        grid=(steps,),
        in_specs=[row_blk(d), _resident(wq.shape), _resident(wk.shape), _resident(wv.shape),
                  _resident((1, HEAD_DIM)), _resident((1, HEAD_DIM))] + cast_in,
        out_specs=[row_blk(aw)] * 3 + cast_out,
        out_shape=[jax.ShapeDtypeStruct((m, aw), BF16)] * 3 + cast_shapes,
        compiler_params=pltpu.CompilerParams(
            dimension_semantics=("arbitrary",),
            vmem_limit_bytes=48 * MIB),
        name="qkv_proj",
    )(h2, wq, wk, wv, q_g.reshape(1, HEAD_DIM), k_g.reshape(1, HEAD_DIM), *cast_args)
    return outs[0], outs[1], outs[2], outs[3:]


def _conv_kernel(vg_ref, halo_ref, cw_ref, cb_ref, lng_ref, lnb_ref, og_ref, o_ref,
                 hb_ref, hs_ref, y_ref, *, tiles_per_seq):
    groups = y_ref.shape[0]
    ts = y_ref.shape[1]
    width = groups * LANES
    first_tile = pl.program_id(0) % tiles_per_seq == 0
    first = HALO - (CONV_KERNEL - 1)

    def group_body(c, carry):
        halo = halo_ref[c].astype(F32) * jax.nn.sigmoid(halo_ref[groups + c].astype(F32))
        hb_ref[0:HALO, :] = jnp.where(first_tile, 0.0, halo)
        hb_ref[HALO:HALO + ts, :] = vg_ref[c].astype(F32) * jax.nn.sigmoid(vg_ref[groups + c].astype(F32))
        hb_ref[HALO + ts:, :] = jnp.zeros((BF16_ROWS, LANES), F32)
        for p in range(BF16_ROWS):
            hs_ref[p, :, :] = hb_ref[p:p + HALO + ts, :].astype(BF16)
        for r in range(0, ts, CONV_ROWS):
            prods = []
            for t in range(CONV_KERNEL):
                a, p = divmod(first + t, BF16_ROWS)
                lo = r + a * BF16_ROWS
                w = jnp.concatenate([cw_ref[c, t]] * (CONV_ROWS // BF16_ROWS), axis=0)
                prods.append(w.astype(F32) * hs_ref[p, lo:lo + CONV_ROWS, :].astype(F32))
            y_ref[c, r:r + CONV_ROWS, :] = functools.reduce(lambda u, v: u + v, prods) + cb_ref[c]
        return carry

    lax.fori_loop(0, groups, group_body, 0)

    for r in range(0, ts, CONV_ROWS):
        rows = slice(r, r + CONV_ROWS)
        y = y_ref[:, rows, :]
        mu = jnp.sum(jnp.sum(y, axis=0), axis=-1, keepdims=True) / width
        yc = y - mu[None]
        var = jnp.sum(jnp.sum(yc * yc, axis=0), axis=-1, keepdims=True) / width
        z = yc * lax.rsqrt(var + EPS)[None] * lng_ref[...] + lnb_ref[...]
        sw = z * jax.nn.sigmoid(z)
        ms = jnp.sum(jnp.sum(sw * sw, axis=0), axis=-1, keepdims=True) / width
        out = sw * lax.rsqrt(ms + EPS)[None] * og_ref[...]
        for c in range(groups):
            o_ref[rows, c * LANES:(c + 1) * LANES] = out[c].astype(o_ref.dtype)


def _conv_branch(vg3, w_dw, b_dw, ln_g, ln_b, out_g, seq_len):
    n_groups2, m, _ = vg3.shape
    cw = w_dw.shape[1]
    groups = cw // LANES
    ts = ROW_TILE
    w_g = w_dw.reshape(CONV_KERNEL, groups, LANES).transpose(1, 0, 2)
    w_g = jnp.broadcast_to(w_g[:, :, None, :], (groups, CONV_KERNEL, BF16_ROWS, LANES)).astype(BF16)
    as_groups = lambda v: v.reshape(groups, 1, LANES)
    per_halo = ts // HALO
    return pl.pallas_call(
        functools.partial(_conv_kernel, tiles_per_seq=seq_len // ts),
        grid=(m // ts,),
        in_specs=[
            pl.BlockSpec((n_groups2, ts, LANES), lambda i: (0, i, 0)),
            pl.BlockSpec((n_groups2, HALO, LANES), lambda i: (0, jnp.maximum(i * per_halo - 1, 0), 0)),
            _resident((groups, CONV_KERNEL, BF16_ROWS, LANES)),
            _resident((groups, 1, LANES)), _resident((groups, 1, LANES)),
            _resident((groups, 1, LANES)), _resident((groups, 1, LANES)),
        ],
        out_specs=pl.BlockSpec((ts, cw), lambda i: (i, 0)),
        out_shape=jax.ShapeDtypeStruct((m, cw), BF16),
        scratch_shapes=[pltpu.VMEM((HALO + ts + BF16_ROWS, LANES), F32),
                        pltpu.VMEM((BF16_ROWS, HALO + ts, LANES), BF16),
                        pltpu.VMEM((groups, ts, LANES), F32)],
        compiler_params=pltpu.CompilerParams(
            dimension_semantics=("arbitrary",),
            vmem_limit_bytes=32 * MIB),
        name="conv_branch",
    )(vg3, vg3, w_g, as_groups(b_dw), as_groups(ln_g), as_groups(ln_b), as_groups(out_g))


def _rel_bias_kernel(row_ref, mask_ref, o_ref):
    width = row_ref.shape[-1]
    x = jnp.broadcast_to(row_ref[0] * LOG2E, (Q_BLOCK, width))
    x = pltpu.roll(x, 0, 1, stride=1, stride_axis=0)
    bias = x[:, Q_BLOCK:] + mask_ref[...]
    col = lax.broadcasted_iota(jnp.int32, bias.shape, 1)
    for t in range(KV_BLOCKS):
        o_ref[t, 0] = jnp.where(col < (KV_BLOCKS - 1 - t) * Q_BLOCK, MASK_VALUE, bias)


def _band_mask():
    r = np.arange(Q_BLOCK)[:, None] // CHUNK
    c = np.arange(KV_BLOCKS * Q_BLOCK)[None, :] // CHUNK
    visible = (c >= r) & (c <= r + LEFT_CHUNKS)
    return np.where(visible, 0.0, MASK_VALUE).astype(np.float32)


def _rel_bias(rel_table):
    h, n_rel = rel_table.shape
    kw = KV_BLOCKS * Q_BLOCK
    width = Q_BLOCK + kw
    left = kw - MAX_REL_DIST
    right = width - left - n_rel
    row = jnp.pad(rel_table[:, ::-1], ((0, 0), (left, right)), mode="edge")
    row = row.reshape(h, 1, width)
    return pl.pallas_call(
        _rel_bias_kernel,
        grid=(h,),
        in_specs=[pl.BlockSpec((1, 1, width), lambda i: (i, 0, 0)),
                  pl.BlockSpec((Q_BLOCK, kw), lambda i: (0, 0))],
        out_specs=pl.BlockSpec((KV_BLOCKS, 1, Q_BLOCK, kw), lambda i: (0, i, 0, 0)),
        out_shape=jax.ShapeDtypeStruct((KV_BLOCKS, h, Q_BLOCK, kw), F32),
        compiler_params=pltpu.CompilerParams(dimension_semantics=("arbitrary",)),
        name="rel_bias",
    )(row, jnp.asarray(_band_mask()))


def _attn_kernel(q_ref, k0_ref, k1_ref, k2_ref, v0_ref, v1_ref, v2_ref, bias_ref,
                 og_ref, o_ref, acc_ref, vext_ref):
    k_refs = (k0_ref, k1_ref, k2_ref)
    v_refs = (v0_ref, v1_ref, v2_ref)
    n_heads = acc_ref.shape[1] // HEAD_DIM
    nt = (((1,), (1,)), ((), ()))

    @pl.when((pl.program_id(0) == 0) & (pl.program_id(1) == 0))
    def _():
        vext_ref[:, :, HEAD_DIM:] = jnp.ones(vext_ref.shape[:2] + (HEAD_DIM,), BF16)

    for h in range(n_heads):
        sl = slice(h * HEAD_DIM, (h + 1) * HEAD_DIM)
        q = q_ref[0, :, sl]
        scores = []
        for j in range(KV_BLOCKS):
            s = lax.dot_general(q, k_refs[j][0, :, sl], nt, preferred_element_type=F32)
            scores.append(s + bias_ref[0, h, :, j * Q_BLOCK:(j + 1) * Q_BLOCK])
            vext_ref[h, j * Q_BLOCK:(j + 1) * Q_BLOCK, :HEAD_DIM] = v_refs[j][0, :, sl]
        m = functools.reduce(jnp.maximum, [jnp.max(s, axis=-1, keepdims=True) for s in scores])
        p = jnp.concatenate([jnp.exp2(s - m).astype(BF16) for s in scores], axis=1)
        o = jnp.dot(p, vext_ref[h], preferred_element_type=F32)
        acc_ref[:, sl] = o[:, :HEAD_DIM] / o[:, HEAD_DIM:]
    y = acc_ref[...]
    o_ref[0] = (y * _rms_scale(y) * og_ref[...]).astype(o_ref.dtype)


def _attention(q3, k3, v3, bias, out_g):
    b, s, aw = q3.shape
    h = bias.shape[1]
    kv_map = lambda j: (lambda bi, i: (bi, jnp.maximum(i + j - (KV_BLOCKS - 1), 0), 0))
    blk = (1, Q_BLOCK, aw)
    in_specs = [pl.BlockSpec(blk, lambda bi, i: (bi, i, 0))]
    in_specs += [pl.BlockSpec(blk, kv_map(j)) for j in range(KV_BLOCKS)]
    in_specs += [pl.BlockSpec(blk, kv_map(j)) for j in range(KV_BLOCKS)]
    in_specs += [
        pl.BlockSpec((1, h, Q_BLOCK, KV_BLOCKS * Q_BLOCK),
                     lambda bi, i: (jnp.minimum(i, KV_BLOCKS - 1), 0, 0, 0)),
        pl.BlockSpec((1, aw), lambda bi, i: (0, 0)),
    ]
    return pl.pallas_call(
        _attn_kernel,
        grid=(b, s // Q_BLOCK),
        in_specs=in_specs,
        out_specs=pl.BlockSpec(blk, lambda bi, i: (bi, i, 0)),
        out_shape=jax.ShapeDtypeStruct((b, s, aw), BF16),
        scratch_shapes=[pltpu.VMEM((Q_BLOCK, aw), F32),
                        pltpu.VMEM((h, KV_BLOCKS * Q_BLOCK, 2 * HEAD_DIM), BF16)],
        compiler_params=pltpu.CompilerParams(
            dimension_semantics=("arbitrary", "arbitrary"),
            vmem_limit_bytes=40 * MIB),
        name="attention",
    )(q3, k3, k3, k3, v3, v3, v3, bias, out_g.reshape(1, aw))


def _out_proj_kernel(x_ref, yc_ref, ya_ref, wc_ref, wa_ref, g_ref, o_ref, h_ref):
    for s in range(x_ref.shape[0] // SUB_ROWS):
        rows = slice(s * SUB_ROWS, (s + 1) * SUB_ROWS)
        acc = jnp.dot(yc_ref[rows, :], wc_ref[...], preferred_element_type=F32)
        acc = acc + jnp.dot(ya_ref[rows, :], wa_ref[...], preferred_element_type=F32)
        x1 = x_ref[rows, :] + acc
        o_ref[rows, :] = x1
        h_ref[rows, :] = (x1 * _rms_scale(x1) * g_ref[...]).astype(h_ref.dtype)


def _out_proj(x2, yc2, ya2, w_out, g2):
    m, d = x2.shape
    cw = yc2.shape[1]
    aw = ya2.shape[1]
    tm = ROW_TILE
    row_blk = lambda width: pl.BlockSpec((tm, width), lambda i: (i, 0))
    return pl.pallas_call(
        _out_proj_kernel,
        grid=(m // tm,),
        in_specs=[row_blk(d), row_blk(cw), row_blk(aw),
                  _resident((cw, d), (0, 0)), _resident((aw, d), (cw // aw, 0)),
                  _resident((1, d))],
        out_specs=[row_blk(d), row_blk(d)],
        out_shape=[jax.ShapeDtypeStruct((m, d), F32), jax.ShapeDtypeStruct((m, d), BF16)],
        compiler_params=pltpu.CompilerParams(
            dimension_semantics=("arbitrary",),
            vmem_limit_bytes=48 * MIB),
        name="out_proj",
    )(x2, yc2, ya2, w_out, w_out, g2)


def _ffn_kernel(x_ref, h_ref, w1_ref, w2_ref, o_ref):
    @pl.when(pl.program_id(1) == 0)
    def _():
        o_ref[...] = x_ref[...]

    a = jnp.dot(h_ref[...], w1_ref[...], preferred_element_type=F32)
    a = jnp.maximum(a, 0.0)
    a = (a * a).astype(BF16)
    o_ref[...] += jnp.dot(a, w2_ref[...], preferred_element_type=F32)


def _ffn(x2, h2, w1, w2, tm=512, tf=1024):
    m, d = x2.shape
    ff = w1.shape[1]
    return pl.pallas_call(
        _ffn_kernel,
        grid=(m // tm, ff // tf),
        in_specs=[
            pl.BlockSpec((tm, d), lambda i, f: (i, 0)),
            pl.BlockSpec((tm, d), lambda i, f: (i, 0)),
            pl.BlockSpec((d, tf), lambda i, f: (0, f)),
            pl.BlockSpec((tf, d), lambda i, f: (f, 0)),
        ],
        out_specs=pl.BlockSpec((tm, d), lambda i, f: (i, 0)),
        out_shape=jax.ShapeDtypeStruct((m, d), F32),
        compiler_params=pltpu.CompilerParams(
            dimension_semantics=("arbitrary", "arbitrary"),
            vmem_limit_bytes=48 * MIB),
        name="ffn",
    )(x2, h2, w1, w2)


def kernel(x, ln1_g, w_in, w_dw, b_dw, conv_ln_g, conv_ln_b, q_norm_g, k_norm_g, rel_bias,
           out_norm_conv_g, out_norm_attn_g, w_out, ln2_g, w_ff1, w_ff2):
    b, s, d = x.shape
    depth = w_in.shape[0]
    cw = w_dw.shape[2]
    aw = out_norm_attn_g.shape[1]
    ff = w_ff1.shape[2]
    qcol = 2 * cw // aw
    x2 = x.reshape(b * s, d)
    for l in range(depth):
        w_vg = w_in[l, :, :2 * cw].astype(BF16)
        vg, h, (wq, wk, wv, wo) = _vg_proj(
            x2, ln1_g[l].reshape(1, d), w_vg,
            [(w_in[l], aw, qcol), (w_in[l], aw, qcol + 1), (w_in[l], aw, qcol + 2), (w_out[l], d, 0)])
        q, k, v, (w1, w2) = _qkv_proj(h, wq, wk, wv, q_norm_g[l], k_norm_g[l],
                                      [(w_ff1[l], ff, 0), (w_ff2[l], d, 0)])
        yc = _conv_branch(vg, w_dw[l], b_dw[l], conv_ln_g[l], conv_ln_b[l], out_norm_conv_g[l], s)
        bias = _rel_bias(rel_bias[l])
        ya = _attention(q.reshape(b, s, aw), k.reshape(b, s, aw), v.reshape(b, s, aw),
                        bias, out_norm_attn_g[l])
        x2, h2 = _out_proj(x2, yc, ya.reshape(b * s, aw), wo, ln2_g[l].reshape(1, d))
        x2 = _ffn(x2, h2, w1, w2)
    return x2.reshape(b, s, d)
```

```python
import functools
import math

import jax
import jax.numpy as jnp
import numpy as np
from jax import lax
from jax.experimental import pallas as pl
from jax.experimental.pallas import tpu as pltpu

F32 = jnp.float32
BF16 = jnp.bfloat16

EPS = 1e-6
CHUNK = 64
LEFT_CHUNKS = 8
HEAD_DIM = 128
CONV_KERNEL = 31
MAX_REL_DIST = 128
LANES = 128
BF16_ROWS = 16
MASK_VALUE = -1e30
LOG2E = math.log2(math.e)

Q_BLOCK = 256
KV_BLOCKS = 3
Q_PER_STEP = 2
HALO = 32
ROW_TILE = 512
SUB_ROWS = 256
CONV_ROWS = 64
NORM_ROWS = 64
MXU_COLS = 512
FFN_ROWS = 1024
FFN_COLS = 512
MIB = 1024 * 1024


def _rms_scale(x):
    return lax.rsqrt(jnp.mean(x * x, axis=-1, keepdims=True) + EPS)


def _resident(shape, index=None):
    index = index or (0,) * len(shape)
    return pl.BlockSpec(shape, lambda *_: index, pipeline_mode=pl.Buffered(1))


def _cast_specs(casts, steps):
    in_specs, out_specs, shapes, args = [], [], [], []
    for arr, width, col in casts:
        slab = arr.shape[0] // steps
        in_specs.append(pl.BlockSpec((slab, width), lambda i, col=col: (i, col)))
        out_specs.append(pl.BlockSpec((slab, width), lambda i: (i, 0)))
        shapes.append(jax.ShapeDtypeStruct((arr.shape[0], width), BF16))
        args.append(arr)
    return in_specs, out_specs, shapes, args


def _vg_proj_kernel(x_ref, g_ref, w_ref, *refs, n_cast):
    cast_in = refs[:n_cast]
    vg_ref, h_ref = refs[n_cast:n_cast + 2]
    cast_out = refs[n_cast + 2:]
    tm = x_ref.shape[0]
    n = w_ref.shape[1]

    def norm_rows(r0, nrows):
        x = x_ref[r0:r0 + nrows, :]
        h_ref[r0:r0 + nrows, :] = (x * _rms_scale(x) * g_ref[...]).astype(BF16)

    for r0 in range(0, tm, NORM_ROWS):
        norm_rows(r0, NORM_ROWS)
    for s in range(tm // SUB_ROWS):
        rows = slice(s * SUB_ROWS, (s + 1) * SUB_ROWS)
        for c0 in range(0, n, MXU_COLS):
            y = jnp.dot(h_ref[rows, :], w_ref[:, c0:c0 + MXU_COLS], preferred_element_type=F32)
            for g in range(MXU_COLS // LANES):
                vg_ref[c0 // LANES + g, rows, :] = y[:, g * LANES:(g + 1) * LANES].astype(vg_ref.dtype)
    for src, dst in zip(cast_in, cast_out):
        dst[...] = src[...].astype(dst.dtype)


def _vg_proj(x2, g, w_vg, casts):
    m, d = x2.shape
    n = w_vg.shape[1]
    steps = m // ROW_TILE
    cast_in, cast_out, cast_shapes, cast_args = _cast_specs(casts, steps)
    outs = pl.pallas_call(
        functools.partial(_vg_proj_kernel, n_cast=len(casts)),
        grid=(steps,),
        in_specs=[pl.BlockSpec((ROW_TILE, d), lambda i: (i, 0)),
                  _resident((1, d)), _resident(w_vg.shape)] + cast_in,
        out_specs=[pl.BlockSpec((n // LANES, ROW_TILE, LANES), lambda i: (0, i, 0)),
                   pl.BlockSpec((ROW_TILE, d), lambda i: (i, 0))] + cast_out,
        out_shape=[jax.ShapeDtypeStruct((n // LANES, m, LANES), BF16),
                   jax.ShapeDtypeStruct((m, d), BF16)] + cast_shapes,
        compiler_params=pltpu.CompilerParams(
            dimension_semantics=("arbitrary",),
            vmem_limit_bytes=48 * MIB),
        name="vg_proj",
    )(x2, g, w_vg, *cast_args)
    return outs[0], outs[1], outs[2:]


def _qkv_proj_kernel(h_ref, wq_ref, wk_ref, wv_ref, qg_ref, kg_ref, *refs, n_cast):
    cast_in = refs[:n_cast]
    q_ref, k_ref, v_ref = refs[n_cast:n_cast + 3]
    cast_out = refs[n_cast + 3:]
    tm = h_ref.shape[0]
    aw = q_ref.shape[1]

    def head_norm(y, gain, out_ref, rows):
        for hd in range(aw // HEAD_DIM):
            sl = slice(hd * HEAD_DIM, (hd + 1) * HEAD_DIM)
            blk = y[:, sl]
            out_ref[rows, sl] = (blk * _rms_scale(blk) * gain).astype(out_ref.dtype)

    qg = qg_ref[...] * (HEAD_DIM ** -0.5 * LOG2E)
    kg = kg_ref[...]
    for s in range(tm // SUB_ROWS):
        rows = slice(s * SUB_ROWS, (s + 1) * SUB_ROWS)
        h = h_ref[rows, :]
        head_norm(jnp.dot(h, wq_ref[...], preferred_element_type=F32), qg, q_ref, rows)
        head_norm(jnp.dot(h, wk_ref[...], preferred_element_type=F32), kg, k_ref, rows)
        v_ref[rows, :] = jnp.dot(h, wv_ref[...], preferred_element_type=F32).astype(v_ref.dtype)
    for src, dst in zip(cast_in, cast_out):
        dst[...] = src[...].astype(dst.dtype)


def _qkv_proj(h2, wq, wk, wv, q_g, k_g, casts):
    m, d = h2.shape
    aw = wq.shape[1]
    steps = m // ROW_TILE
    row_blk = lambda width: pl.BlockSpec((ROW_TILE, width), lambda i: (i, 0))
    cast_in, cast_out, cast_shapes, cast_args = _cast_specs(casts, steps)
    outs = pl.pallas_call(
        functools.partial(_qkv_proj_kernel, n_cast=len(casts)),
        grid=(steps,),
        in_specs=[row_blk(d), _resident(wq.shape), _resident(wk.shape), _resident(wv.shape),
                  _resident((1, HEAD_DIM)), _resident((1, HEAD_DIM))] + cast_in,
        out_specs=[row_blk(aw)] * 3 + cast_out,
        out_shape=[jax.ShapeDtypeStruct((m, aw), BF16)] * 3 + cast_shapes,
        compiler_params=pltpu.CompilerParams(
            dimension_semantics=("arbitrary",),
            vmem_limit_bytes=48 * MIB),
        name="qkv_proj",
    )(h2, wq, wk, wv, q_g.reshape(1, HEAD_DIM), k_g.reshape(1, HEAD_DIM), *cast_args)
    return outs[0], outs[1], outs[2], outs[3:]


def _conv_kernel(vg_ref, halo_ref, cw_ref, cb_ref, lng_ref, lnb_ref, og_ref, o_ref,
                 hb_ref, hs_ref, y_ref, *, tiles_per_seq):
    groups = y_ref.shape[0]
    ts = y_ref.shape[1]
    width = groups * LANES
    first_tile = pl.program_id(0) % tiles_per_seq == 0
    first = HALO - (CONV_KERNEL - 1)

    def group_body(c, carry):
        halo = halo_ref[c].astype(F32) * jax.nn.sigmoid(halo_ref[groups + c].astype(F32))
        hb_ref[0:HALO, :] = jnp.where(first_tile, 0.0, halo)
        hb_ref[HALO:HALO + ts, :] = vg_ref[c].astype(F32) * jax.nn.sigmoid(vg_ref[groups + c].astype(F32))
        hb_ref[HALO + ts:, :] = jnp.zeros((BF16_ROWS, LANES), F32)
        for p in range(BF16_ROWS):
            hs_ref[p, :, :] = hb_ref[p:p + HALO + ts, :].astype(BF16)
        for r in range(0, ts, CONV_ROWS):
            prods = []
            for t in range(CONV_KERNEL):
                a, p = divmod(first + t, BF16_ROWS)
                lo = r + a * BF16_ROWS
                w = jnp.concatenate([cw_ref[c, t]] * (CONV_ROWS // BF16_ROWS), axis=0)
                prods.append(w.astype(F32) * hs_ref[p, lo:lo + CONV_ROWS, :].astype(F32))
            y_ref[c, r:r + CONV_ROWS, :] = functools.reduce(lambda u, v: u + v, prods) + cb_ref[c]
        return carry

    lax.fori_loop(0, groups, group_body, 0)

    for r in range(0, ts, CONV_ROWS):
        rows = slice(r, r + CONV_ROWS)
        y = y_ref[:, rows, :]
        mu = jnp.sum(jnp.sum(y, axis=0), axis=-1, keepdims=True) / width
        yc = y - mu[None]
        var = jnp.sum(jnp.sum(yc * yc, axis=0), axis=-1, keepdims=True) / width
        z = yc * lax.rsqrt(var + EPS)[None] * lng_ref[...] + lnb_ref[...]
        sw = z * jax.nn.sigmoid(z)
        ms = jnp.sum(jnp.sum(sw * sw, axis=0), axis=-1, keepdims=True) / width
        out = sw * lax.rsqrt(ms + EPS)[None] * og_ref[...]
        for c in range(groups):
            o_ref[rows, c * LANES:(c + 1) * LANES] = out[c].astype(o_ref.dtype)


def _conv_branch(vg3, w_dw, b_dw, ln_g, ln_b, out_g, seq_len):
    n_groups2, m, _ = vg3.shape
    cw = w_dw.shape[1]
    groups = cw // LANES
    ts = ROW_TILE
    w_g = w_dw.reshape(CONV_KERNEL, groups, LANES).transpose(1, 0, 2)
    w_g = jnp.broadcast_to(w_g[:, :, None, :], (groups, CONV_KERNEL, BF16_ROWS, LANES)).astype(BF16)
    as_groups = lambda v: v.reshape(groups, 1, LANES)
    per_halo = ts // HALO
    return pl.pallas_call(
        functools.partial(_conv_kernel, tiles_per_seq=seq_len // ts),
        grid=(m // ts,),
        in_specs=[
            pl.BlockSpec((n_groups2, ts, LANES), lambda i: (0, i, 0)),
            pl.BlockSpec((n_groups2, HALO, LANES), lambda i: (0, jnp.maximum(i * per_halo - 1, 0), 0)),
            _resident((groups, CONV_KERNEL, BF16_ROWS, LANES)),
            _resident((groups, 1, LANES)), _resident((groups, 1, LANES)),
            _resident((groups, 1, LANES)), _resident((groups, 1, LANES)),
        ],
        out_specs=pl.BlockSpec((ts, cw), lambda i: (i, 0)),
        out_shape=jax.ShapeDtypeStruct((m, cw), BF16),
        scratch_shapes=[pltpu.VMEM((HALO + ts + BF16_ROWS, LANES), F32),
                        pltpu.VMEM((BF16_ROWS, HALO + ts, LANES), BF16),
                        pltpu.VMEM((groups, ts, LANES), F32)],
        compiler_params=pltpu.CompilerParams(
            dimension_semantics=("arbitrary",),
            vmem_limit_bytes=32 * MIB),
        name="conv_branch",
    )(vg3, vg3, w_g, as_groups(b_dw), as_groups(ln_g), as_groups(ln_b), as_groups(out_g))


def _rel_bias_kernel(row_ref, mask_ref, o_ref):
    width = row_ref.shape[-1]
    x = jnp.broadcast_to(row_ref[0] * LOG2E, (Q_BLOCK, width))
    x = pltpu.roll(x, 0, 1, stride=1, stride_axis=0)
    bias = x[:, Q_BLOCK:] + mask_ref[...]
    col = lax.broadcasted_iota(jnp.int32, bias.shape, 1)
    for t in range(KV_BLOCKS):
        o_ref[t, 0] = jnp.where(col < (KV_BLOCKS - 1 - t) * Q_BLOCK, MASK_VALUE, bias)


def _band_mask():
    r = np.arange(Q_BLOCK)[:, None] // CHUNK
    c = np.arange(KV_BLOCKS * Q_BLOCK)[None, :] // CHUNK
    visible = (c >= r) & (c <= r + LEFT_CHUNKS)
    return np.where(visible, 0.0, MASK_VALUE).astype(np.float32)


def _rel_bias(rel_table):
    h, n_rel = rel_table.shape
    kw = KV_BLOCKS * Q_BLOCK
    width = Q_BLOCK + kw
    left = kw - MAX_REL_DIST
    right = width - left - n_rel
    row = jnp.pad(rel_table[:, ::-1], ((0, 0), (left, right)), mode="edge")
    row = row.reshape(h, 1, width)
    return pl.pallas_call(
        _rel_bias_kernel,
        grid=(h,),
        in_specs=[pl.BlockSpec((1, 1, width), lambda i: (i, 0, 0)),
                  pl.BlockSpec((Q_BLOCK, kw), lambda i: (0, 0))],
        out_specs=pl.BlockSpec((KV_BLOCKS, 1, Q_BLOCK, kw), lambda i: (0, i, 0, 0)),
        out_shape=jax.ShapeDtypeStruct((KV_BLOCKS, h, Q_BLOCK, kw), F32),
        compiler_params=pltpu.CompilerParams(dimension_semantics=("arbitrary",)),
        name="rel_bias",
    )(row, jnp.asarray(_band_mask()))


def _attn_kernel(q_ref, kp_ref, kc_ref, vp_ref, vc_ref, bias_ref, og_ref, o_ref,
                 acc_ref, vext_ref):
    i = pl.program_id(1)
    n_heads = acc_ref.shape[1] // HEAD_DIM
    nt = (((1,), (1,)), ((), ()))
    win = KV_BLOCKS * Q_BLOCK
    tile = Q_PER_STEP * Q_BLOCK

    @pl.when((pl.program_id(0) == 0) & (i == 0))
    def _():
        vext_ref[:, :, HEAD_DIM:] = jnp.ones(vext_ref.shape[:2] + (HEAD_DIM,), BF16)

    for h in range(n_heads):
        sl = slice(h * HEAD_DIM, (h + 1) * HEAD_DIM)
        vext_ref[h, :tile, :HEAD_DIM] = vp_ref[0, :, sl]
        vext_ref[h, tile:, :HEAD_DIM] = vc_ref[0, :, sl]

    def key_block(n, sl):
        ref = kp_ref if n < Q_PER_STEP else kc_ref
        r0 = (n % Q_PER_STEP) * Q_BLOCK
        return ref[0, r0:r0 + Q_BLOCK, sl]

    for qb in range(Q_PER_STEP):
        rows = slice(qb * Q_BLOCK, (qb + 1) * Q_BLOCK)
        variant = jnp.minimum(i * Q_PER_STEP + qb, KV_BLOCKS - 1)
        for h in range(n_heads):
            sl = slice(h * HEAD_DIM, (h + 1) * HEAD_DIM)
            q = q_ref[0, rows, sl]
            scores = []
            for j in range(KV_BLOCKS):
                s = lax.dot_general(q, key_block(qb + j, sl), nt, preferred_element_type=F32)
                scores.append(s + bias_ref[variant, h, :, j * Q_BLOCK:(j + 1) * Q_BLOCK])
            m = functools.reduce(jnp.maximum, [jnp.max(s, axis=-1, keepdims=True) for s in scores])
            p = jnp.concatenate([jnp.exp2(s - m).astype(BF16) for s in scores], axis=1)
            o = jnp.dot(p, vext_ref[h, qb * Q_BLOCK:qb * Q_BLOCK + win, :],
                        preferred_element_type=F32)
            acc_ref[rows, sl] = o[:, :HEAD_DIM] / o[:, HEAD_DIM:]
        y = acc_ref[rows, :]
        o_ref[0, rows, :] = (y * _rms_scale(y) * og_ref[...]).astype(o_ref.dtype)


def _attention(q3, k3, v3, bias, out_g):
    b, s, aw = q3.shape
    h = bias.shape[1]
    tile = Q_PER_STEP * Q_BLOCK
    blk = (1, tile, aw)
    cur = pl.BlockSpec(blk, lambda bi, i: (bi, i, 0))
    prev = pl.BlockSpec(blk, lambda bi, i: (bi, jnp.maximum(i - 1, 0), 0))
    return pl.pallas_call(
        _attn_kernel,
        grid=(b, s // tile),
        in_specs=[cur, prev, cur, prev, cur, _resident(bias.shape), _resident((1, aw))],
        out_specs=pl.BlockSpec(blk, lambda bi, i: (bi, i, 0)),
        out_shape=jax.ShapeDtypeStruct((b, s, aw), BF16),
        scratch_shapes=[pltpu.VMEM((tile, aw), F32),
                        pltpu.VMEM((h, 2 * tile, 2 * HEAD_DIM), BF16)],
        compiler_params=pltpu.CompilerParams(
            dimension_semantics=("arbitrary", "arbitrary"),
            vmem_limit_bytes=48 * MIB),
        name="attention",
    )(q3, k3, k3, v3, v3, bias, out_g.reshape(1, aw))


def _out_proj_kernel(x_ref, yc_ref, ya_ref, wc_ref, wa_ref, o_ref):
    acc = jnp.dot(yc_ref[...], wc_ref[...], preferred_element_type=F32)
    acc = acc + jnp.dot(ya_ref[...], wa_ref[...], preferred_element_type=F32)
    o_ref[...] = x_ref[...] + acc


def _out_proj(x2, yc2, ya2, w_out):
    m, d = x2.shape
    cw = yc2.shape[1]
    aw = ya2.shape[1]
    tm = ROW_TILE
    row_blk = lambda width: pl.BlockSpec((tm, width), lambda i: (i, 0))
    return pl.pallas_call(
        _out_proj_kernel,
        grid=(m // tm,),
        in_specs=[row_blk(d), row_blk(cw), row_blk(aw),
                  _resident((cw, d), (0, 0)), _resident((aw, d), (cw // aw, 0))],
        out_specs=row_blk(d),
        out_shape=jax.ShapeDtypeStruct((m, d), F32),
        compiler_params=pltpu.CompilerParams(
            dimension_semantics=("arbitrary",),
            vmem_limit_bytes=48 * MIB),
        name="out_proj",
    )(x2, yc2, ya2, w_out, w_out)


def _ffn_kernel(x_ref, g_ref, w1_ref, w2_ref, o_ref, h_ref):
    @pl.when(pl.program_id(1) == 0)
    def _():
        for r0 in range(0, x_ref.shape[0], SUB_ROWS):
            x = x_ref[r0:r0 + SUB_ROWS, :]
            h_ref[r0:r0 + SUB_ROWS, :] = (x * _rms_scale(x) * g_ref[...]).astype(BF16)
            o_ref[r0:r0 + SUB_ROWS, :] = x

    a = jnp.dot(h_ref[...], w1_ref[...], preferred_element_type=F32)
    a = jnp.maximum(a, 0.0)
    a = (a * a).astype(BF16)
    o_ref[...] += jnp.dot(a, w2_ref[...], preferred_element_type=F32)


def _ffn(x2, g, w1, w2):
    m, d = x2.shape
    ff = w1.shape[1]
    tm, tf = FFN_ROWS, FFN_COLS
    return pl.pallas_call(
        _ffn_kernel,
        grid=(m // tm, ff // tf),
        in_specs=[
            pl.BlockSpec((tm, d), lambda i, f: (i, 0)),
            _resident((1, d)),
            pl.BlockSpec((d, tf), lambda i, f: (0, f)),
            pl.BlockSpec((tf, d), lambda i, f: (f, 0)),
        ],
        out_specs=pl.BlockSpec((tm, d), lambda i, f: (i, 0)),
        out_shape=jax.ShapeDtypeStruct((m, d), F32),
        scratch_shapes=[pltpu.VMEM((tm, d), BF16)],
        compiler_params=pltpu.CompilerParams(
            dimension_semantics=("arbitrary", "arbitrary"),
            vmem_limit_bytes=56 * MIB),
        name="ffn",
    )(x2, g, w1, w2)


def kernel(x, ln1_g, w_in, w_dw, b_dw, conv_ln_g, conv_ln_b, q_norm_g, k_norm_g, rel_bias,
           out_norm_conv_g, out_norm_attn_g, w_out, ln2_g, w_ff1, w_ff2):
    b, s, d = x.shape
    depth = w_in.shape[0]
    cw = w_dw.shape[2]
    aw = out_norm_attn_g.shape[1]
    ff = w_ff1.shape[2]
    qcol = 2 * cw // aw
    x2 = x.reshape(b * s, d)
    for l in range(depth):
        w_vg = w_in[l, :, :2 * cw].astype(BF16)
        vg, h, (wq, wk, wv, wo) = _vg_proj(
            x2, ln1_g[l].reshape(1, d), w_vg,
            [(w_in[l], aw, qcol), (w_in[l], aw, qcol + 1), (w_in[l], aw, qcol + 2), (w_out[l], d, 0)])
        q, k, v, (w1, w2) = _qkv_proj(h, wq, wk, wv, q_norm_g[l], k_norm_g[l],
                                      [(w_ff1[l], ff, 0), (w_ff2[l], d, 0)])
        yc = _conv_branch(vg, w_dw[l], b_dw[l], conv_ln_g[l], conv_ln_b[l], out_norm_conv_g[l], s)
        bias = _rel_bias(rel_bias[l])
        ya = _attention(q.reshape(b, s, aw), k.reshape(b, s, aw), v.reshape(b, s, aw),
                        bias, out_norm_attn_g[l])
        x2 = _out_proj(x2, yc, ya.reshape(b * s, aw), wo)
        x2 = _ffn(x2, ln2_g[l].reshape(1, d), w1, w2)
    return x2.reshape(b, s, d)
```

```python
import functools
import math

import jax
import jax.numpy as jnp
import numpy as np
from jax import lax
from jax.experimental import pallas as pl
from jax.experimental.pallas import tpu as pltpu

F32 = jnp.float32
BF16 = jnp.bfloat16

EPS = 1e-6
CHUNK = 64
LEFT_CHUNKS = 8
HEAD_DIM = 128
CONV_KERNEL = 31
MAX_REL_DIST = 128
LANES = 128
BF16_ROWS = 16
MASK_VALUE = -1e30
LOG2E = math.log2(math.e)

Q_BLOCK = 256
KV_BLOCKS = 3
Q_PER_STEP = 2
HALO = 32
ROW_TILE = 512
SUB_ROWS = 256
CONV_ROWS = 64
NORM_ROWS = 64
MXU_COLS = 512
FFN_ROWS = 512
FFN_COLS = 2048
FFN_CHUNK = 512
MIB = 1024 * 1024


def _rms_scale(x):
    return lax.rsqrt(jnp.mean(x * x, axis=-1, keepdims=True) + EPS)


def _resident(shape, index=None):
    index = index or (0,) * len(shape)
    return pl.BlockSpec(shape, lambda *_: index, pipeline_mode=pl.Buffered(1))


def _cast_specs(casts, steps):
    in_specs, out_specs, shapes, args = [], [], [], []
    for arr, width, col in casts:
        slab = arr.shape[0] // steps
        in_specs.append(pl.BlockSpec((slab, width), lambda i, col=col: (i, col)))
        out_specs.append(pl.BlockSpec((slab, width), lambda i: (i, 0)))
        shapes.append(jax.ShapeDtypeStruct((arr.shape[0], width), BF16))
        args.append(arr)
    return in_specs, out_specs, shapes, args


def _vg_proj_kernel(x_ref, g_ref, w_ref, *refs, n_cast):
    cast_in = refs[:n_cast]
    vg_ref, h_ref = refs[n_cast:n_cast + 2]
    cast_out = refs[n_cast + 2:]
    tm = x_ref.shape[0]
    n = w_ref.shape[1]

    def norm_rows(r0, nrows):
        x = x_ref[r0:r0 + nrows, :]
        h_ref[r0:r0 + nrows, :] = (x * _rms_scale(x) * g_ref[...]).astype(BF16)

    for r0 in range(0, tm, NORM_ROWS):
        norm_rows(r0, NORM_ROWS)
    for s in range(tm // SUB_ROWS):
        rows = slice(s * SUB_ROWS, (s + 1) * SUB_ROWS)
        for c0 in range(0, n, MXU_COLS):
            y = jnp.dot(h_ref[rows, :], w_ref[:, c0:c0 + MXU_COLS], preferred_element_type=F32)
            for g in range(MXU_COLS // LANES):
                vg_ref[c0 // LANES + g, rows, :] = y[:, g * LANES:(g + 1) * LANES].astype(vg_ref.dtype)
    for src, dst in zip(cast_in, cast_out):
        dst[...] = src[...].astype(dst.dtype)


def _vg_proj(x2, g, w_vg, casts):
    m, d = x2.shape
    n = w_vg.shape[1]
    steps = m // ROW_TILE
    cast_in, cast_out, cast_shapes, cast_args = _cast_specs(casts, steps)
    outs = pl.pallas_call(
        functools.partial(_vg_proj_kernel, n_cast=len(casts)),
        grid=(steps,),
        in_specs=[pl.BlockSpec((ROW_TILE, d), lambda i: (i, 0)),
                  _resident((1, d)), _resident(w_vg.shape)] + cast_in,
        out_specs=[pl.BlockSpec((n // LANES, ROW_TILE, LANES), lambda i: (0, i, 0)),
                   pl.BlockSpec((ROW_TILE, d), lambda i: (i, 0))] + cast_out,
        out_shape=[jax.ShapeDtypeStruct((n // LANES, m, LANES), BF16),
                   jax.ShapeDtypeStruct((m, d), BF16)] + cast_shapes,
        compiler_params=pltpu.CompilerParams(
            dimension_semantics=("arbitrary",),
            vmem_limit_bytes=48 * MIB),
        name="vg_proj",
    )(x2, g, w_vg, *cast_args)
    return outs[0], outs[1], outs[2:]


def _qkv_proj_kernel(h_ref, wq_ref, wk_ref, wv_ref, qg_ref, kg_ref, *refs, n_cast):
    cast_in = refs[:n_cast]
    q_ref, k_ref, v_ref = refs[n_cast:n_cast + 3]
    cast_out = refs[n_cast + 3:]
    tm = h_ref.shape[0]
    aw = q_ref.shape[1]

    def head_norm(y, gain, out_ref, rows):
        for hd in range(aw // HEAD_DIM):
            sl = slice(hd * HEAD_DIM, (hd + 1) * HEAD_DIM)
            blk = y[:, sl]
            out_ref[rows, sl] = (blk * _rms_scale(blk) * gain).astype(out_ref.dtype)

    qg = qg_ref[...] * (HEAD_DIM ** -0.5 * LOG2E)
    kg = kg_ref[...]
    for s in range(tm // SUB_ROWS):
        rows = slice(s * SUB_ROWS, (s + 1) * SUB_ROWS)
        h = h_ref[rows, :]
        head_norm(jnp.dot(h, wq_ref[...], preferred_element_type=F32), qg, q_ref, rows)
        head_norm(jnp.dot(h, wk_ref[...], preferred_element_type=F32), kg, k_ref, rows)
        v_ref[rows, :] = jnp.dot(h, wv_ref[...], preferred_element_type=F32).astype(v_ref.dtype)
    for src, dst in zip(cast_in, cast_out):
        dst[...] = src[...].astype(dst.dtype)


def _qkv_proj(h2, wq, wk, wv, q_g, k_g, casts):
    m, d = h2.shape
    aw = wq.shape[1]
    steps = m // ROW_TILE
    row_blk = lambda width: pl.BlockSpec((ROW_TILE, width), lambda i: (i, 0))
    cast_in, cast_out, cast_shapes, cast_args = _cast_specs(casts, steps)
    outs = pl.pallas_call(
        functools.partial(_qkv_proj_kernel, n_cast=len(casts)),
        grid=(steps,),
        in_specs=[row_blk(d), _resident(wq.shape), _resident(wk.shape), _resident(wv.shape),
                  _resident((1, HEAD_DIM)), _resident((1, HEAD_DIM))] + cast_in,
        out_specs=[row_blk(aw)] * 3 + cast_out,
        out_shape=[jax.ShapeDtypeStruct((m, aw), BF16)] * 3 + cast_shapes,
        compiler_params=pltpu.CompilerParams(
            dimension_semantics=("arbitrary",),
            vmem_limit_bytes=48 * MIB),
        name="qkv_proj",
    )(h2, wq, wk, wv, q_g.reshape(1, HEAD_DIM), k_g.reshape(1, HEAD_DIM), *cast_args)
    return outs[0], outs[1], outs[2], outs[3:]


def _conv_kernel(vg_ref, halo_ref, cw_ref, cb_ref, lng_ref, lnb_ref, og_ref, o_ref,
                 hb_ref, hs_ref, y_ref, *, tiles_per_seq):
    groups = y_ref.shape[0]
    ts = y_ref.shape[1]
    width = groups * LANES
    first_tile = pl.program_id(0) % tiles_per_seq == 0
    first = HALO - (CONV_KERNEL - 1)

    def group_body(c, carry):
        halo = halo_ref[c].astype(F32) * jax.nn.sigmoid(halo_ref[groups + c].astype(F32))
        hb_ref[0:HALO, :] = jnp.where(first_tile, 0.0, halo)
        hb_ref[HALO:HALO + ts, :] = vg_ref[c].astype(F32) * jax.nn.sigmoid(vg_ref[groups + c].astype(F32))
        hb_ref[HALO + ts:, :] = jnp.zeros((BF16_ROWS, LANES), F32)
        for p in range(BF16_ROWS):
            hs_ref[p, :, :] = hb_ref[p:p + HALO + ts, :].astype(BF16)
        for r in range(0, ts, CONV_ROWS):
            prods = []
            for t in range(CONV_KERNEL):
                a, p = divmod(first + t, BF16_ROWS)
                lo = r + a * BF16_ROWS
                w = jnp.concatenate([cw_ref[c, t]] * (CONV_ROWS // BF16_ROWS), axis=0)
                prods.append(w.astype(F32) * hs_ref[p, lo:lo + CONV_ROWS, :].astype(F32))
            y_ref[c, r:r + CONV_ROWS, :] = functools.reduce(lambda u, v: u + v, prods) + cb_ref[c]
        return carry

    lax.fori_loop(0, groups, group_body, 0)

    for r in range(0, ts, CONV_ROWS):
        rows = slice(r, r + CONV_ROWS)
        y = y_ref[:, rows, :]
        mu = jnp.sum(jnp.sum(y, axis=0), axis=-1, keepdims=True) / width
        yc = y - mu[None]
        var = jnp.sum(jnp.sum(yc * yc, axis=0), axis=-1, keepdims=True) / width
        z = yc * lax.rsqrt(var + EPS)[None] * lng_ref[...] + lnb_ref[...]
        sw = z * jax.nn.sigmoid(z)
        ms = jnp.sum(jnp.sum(sw * sw, axis=0), axis=-1, keepdims=True) / width
        out = sw * lax.rsqrt(ms + EPS)[None] * og_ref[...]
        for c in range(groups):
            o_ref[rows, c * LANES:(c + 1) * LANES] = out[c].astype(o_ref.dtype)


def _conv_branch(vg3, w_dw, b_dw, ln_g, ln_b, out_g, seq_len):
    n_groups2, m, _ = vg3.shape
    cw = w_dw.shape[1]
    groups = cw // LANES
    ts = ROW_TILE
    w_g = w_dw.reshape(CONV_KERNEL, groups, LANES).transpose(1, 0, 2)
    w_g = jnp.broadcast_to(w_g[:, :, None, :], (groups, CONV_KERNEL, BF16_ROWS, LANES)).astype(BF16)
    as_groups = lambda v: v.reshape(groups, 1, LANES)
    per_halo = ts // HALO
    return pl.pallas_call(
        functools.partial(_conv_kernel, tiles_per_seq=seq_len // ts),
        grid=(m // ts,),
        in_specs=[
            pl.BlockSpec((n_groups2, ts, LANES), lambda i: (0, i, 0)),
            pl.BlockSpec((n_groups2, HALO, LANES), lambda i: (0, jnp.maximum(i * per_halo - 1, 0), 0)),
            _resident((groups, CONV_KERNEL, BF16_ROWS, LANES)),
            _resident((groups, 1, LANES)), _resident((groups, 1, LANES)),
            _resident((groups, 1, LANES)), _resident((groups, 1, LANES)),
        ],
        out_specs=pl.BlockSpec((ts, cw), lambda i: (i, 0)),
        out_shape=jax.ShapeDtypeStruct((m, cw), BF16),
        scratch_shapes=[pltpu.VMEM((HALO + ts + BF16_ROWS, LANES), F32),
                        pltpu.VMEM((BF16_ROWS, HALO + ts, LANES), BF16),
                        pltpu.VMEM((groups, ts, LANES), F32)],
        compiler_params=pltpu.CompilerParams(
            dimension_semantics=("arbitrary",),
            vmem_limit_bytes=32 * MIB),
        name="conv_branch",
    )(vg3, vg3, w_g, as_groups(b_dw), as_groups(ln_g), as_groups(ln_b), as_groups(out_g))


def _rel_bias_kernel(row_ref, mask_ref, o_ref):
    width = row_ref.shape[-1]
    x = jnp.broadcast_to(row_ref[0] * LOG2E, (Q_BLOCK, width))
    x = pltpu.roll(x, 0, 1, stride=1, stride_axis=0)
    bias = x[:, Q_BLOCK:] + mask_ref[...]
    col = lax.broadcasted_iota(jnp.int32, bias.shape, 1)
    for t in range(KV_BLOCKS):
        o_ref[t, 0] = jnp.where(col < (KV_BLOCKS - 1 - t) * Q_BLOCK, MASK_VALUE, bias)


def _band_mask():
    r = np.arange(Q_BLOCK)[:, None] // CHUNK
    c = np.arange(KV_BLOCKS * Q_BLOCK)[None, :] // CHUNK
    visible = (c >= r) & (c <= r + LEFT_CHUNKS)
    return np.where(visible, 0.0, MASK_VALUE).astype(np.float32)


def _rel_bias(rel_table):
    h, n_rel = rel_table.shape
    kw = KV_BLOCKS * Q_BLOCK
    width = Q_BLOCK + kw
    left = kw - MAX_REL_DIST
    right = width - left - n_rel
    row = jnp.pad(rel_table[:, ::-1], ((0, 0), (left, right)), mode="edge")
    row = row.reshape(h, 1, width)
    return pl.pallas_call(
        _rel_bias_kernel,
        grid=(h,),
        in_specs=[pl.BlockSpec((1, 1, width), lambda i: (i, 0, 0)),
                  pl.BlockSpec((Q_BLOCK, kw), lambda i: (0, 0))],
        out_specs=pl.BlockSpec((KV_BLOCKS, 1, Q_BLOCK, kw), lambda i: (0, i, 0, 0)),
        out_shape=jax.ShapeDtypeStruct((KV_BLOCKS, h, Q_BLOCK, kw), F32),
        compiler_params=pltpu.CompilerParams(dimension_semantics=("arbitrary",)),
        name="rel_bias",
    )(row, jnp.asarray(_band_mask()))


def _attn_kernel(q_ref, kp_ref, kc_ref, vp_ref, vc_ref, bias_ref, og_ref, o_ref,
                 acc_ref, vext_ref):
    i = pl.program_id(1)
    n_heads = acc_ref.shape[1] // HEAD_DIM
    nt = (((1,), (1,)), ((), ()))
    win = KV_BLOCKS * Q_BLOCK
    tile = Q_PER_STEP * Q_BLOCK

    @pl.when((pl.program_id(0) == 0) & (i == 0))
    def _():
        vext_ref[:, :, HEAD_DIM:] = jnp.ones(vext_ref.shape[:2] + (HEAD_DIM,), BF16)

    for h in range(n_heads):
        sl = slice(h * HEAD_DIM, (h + 1) * HEAD_DIM)
        vext_ref[h, :tile, :HEAD_DIM] = vp_ref[0, :, sl]
        vext_ref[h, tile:, :HEAD_DIM] = vc_ref[0, :, sl]

    def key_block(n, sl):
        ref = kp_ref if n < Q_PER_STEP else kc_ref
        r0 = (n % Q_PER_STEP) * Q_BLOCK
        return ref[0, r0:r0 + Q_BLOCK, sl]

    for qb in range(Q_PER_STEP):
        rows = slice(qb * Q_BLOCK, (qb + 1) * Q_BLOCK)
        variant = jnp.minimum(i * Q_PER_STEP + qb, KV_BLOCKS - 1)
        for h in range(n_heads):
            sl = slice(h * HEAD_DIM, (h + 1) * HEAD_DIM)
            q = q_ref[0, rows, sl]
            scores = []
            for j in range(KV_BLOCKS):
                s = lax.dot_general(q, key_block(qb + j, sl), nt, preferred_element_type=F32)
                scores.append(s + bias_ref[variant, h, :, j * Q_BLOCK:(j + 1) * Q_BLOCK])
            m = functools.reduce(jnp.maximum, [jnp.max(s, axis=-1, keepdims=True) for s in scores])
            p = jnp.concatenate([jnp.exp2(s - m).astype(BF16) for s in scores], axis=1)
            o = jnp.dot(p, vext_ref[h, qb * Q_BLOCK:qb * Q_BLOCK + win, :],
                        preferred_element_type=F32)
            acc_ref[rows, sl] = o[:, :HEAD_DIM] / o[:, HEAD_DIM:]
        y = acc_ref[rows, :]
        o_ref[0, rows, :] = (y * _rms_scale(y) * og_ref[...]).astype(o_ref.dtype)


def _attention(q3, k3, v3, bias, out_g):
    b, s, aw = q3.shape
    h = bias.shape[1]
    tile = Q_PER_STEP * Q_BLOCK
    blk = (1, tile, aw)
    cur = pl.BlockSpec(blk, lambda bi, i: (bi, i, 0))
    prev = pl.BlockSpec(blk, lambda bi, i: (bi, jnp.maximum(i - 1, 0), 0))
    return pl.pallas_call(
        _attn_kernel,
        grid=(b, s // tile),
        in_specs=[cur, prev, cur, prev, cur, _resident(bias.shape), _resident((1, aw))],
        out_specs=pl.BlockSpec(blk, lambda bi, i: (bi, i, 0)),
        out_shape=jax.ShapeDtypeStruct((b, s, aw), BF16),
        scratch_shapes=[pltpu.VMEM((tile, aw), F32),
                        pltpu.VMEM((h, 2 * tile, 2 * HEAD_DIM), BF16)],
        compiler_params=pltpu.CompilerParams(
            dimension_semantics=("arbitrary", "arbitrary"),
            vmem_limit_bytes=48 * MIB),
        name="attention",
    )(q3, k3, k3, v3, v3, bias, out_g.reshape(1, aw))


def _out_proj_kernel(x_ref, yc_ref, ya_ref, wc_ref, wa_ref, o_ref):
    acc = jnp.dot(yc_ref[...], wc_ref[...], preferred_element_type=F32)
    acc = acc + jnp.dot(ya_ref[...], wa_ref[...], preferred_element_type=F32)
    o_ref[...] = x_ref[...] + acc


def _out_proj(x2, yc2, ya2, w_out):
    m, d = x2.shape
    cw = yc2.shape[1]
    aw = ya2.shape[1]
    tm = ROW_TILE
    row_blk = lambda width: pl.BlockSpec((tm, width), lambda i: (i, 0))
    return pl.pallas_call(
        _out_proj_kernel,
        grid=(m // tm,),
        in_specs=[row_blk(d), row_blk(cw), row_blk(aw),
                  _resident((cw, d), (0, 0)), _resident((aw, d), (cw // aw, 0))],
        out_specs=row_blk(d),
        out_shape=jax.ShapeDtypeStruct((m, d), F32),
        compiler_params=pltpu.CompilerParams(
            dimension_semantics=("arbitrary",),
            vmem_limit_bytes=48 * MIB),
        name="out_proj",
    )(x2, yc2, ya2, w_out, w_out)


def _ffn_kernel(x_ref, g_ref, w1_ref, w2_ref, o_ref, h_ref):
    @pl.when(pl.program_id(1) == 0)
    def _():
        for r0 in range(0, x_ref.shape[0], SUB_ROWS):
            x = x_ref[r0:r0 + SUB_ROWS, :]
            h_ref[r0:r0 + SUB_ROWS, :] = (x * _rms_scale(x) * g_ref[...]).astype(BF16)
            o_ref[r0:r0 + SUB_ROWS, :] = x

    for c0 in range(0, w1_ref.shape[1], FFN_CHUNK):
        a = jnp.dot(h_ref[...], w1_ref[:, c0:c0 + FFN_CHUNK], preferred_element_type=F32)
        a = jnp.maximum(a, 0.0)
        a = (a * a).astype(BF16)
        o_ref[...] += jnp.dot(a, w2_ref[c0:c0 + FFN_CHUNK, :], preferred_element_type=F32)


def _ffn(x2, g, w1, w2):
    m, d = x2.shape
    ff = w1.shape[1]
    tm, tf = FFN_ROWS, FFN_COLS
    return pl.pallas_call(
        _ffn_kernel,
        grid=(m // tm, ff // tf),
        in_specs=[
            pl.BlockSpec((tm, d), lambda i, f: (i, 0)),
            _resident((1, d)),
            pl.BlockSpec((d, tf), lambda i, f: (0, f)),
            pl.BlockSpec((tf, d), lambda i, f: (f, 0)),
        ],
        out_specs=pl.BlockSpec((tm, d), lambda i, f: (i, 0)),
        out_shape=jax.ShapeDtypeStruct((m, d), F32),
        scratch_shapes=[pltpu.VMEM((tm, d), BF16)],
        compiler_params=pltpu.CompilerParams(
            dimension_semantics=("arbitrary", "arbitrary"),
            vmem_limit_bytes=56 * MIB),
        name="ffn",
    )(x2, g, w1, w2)


def kernel(x, ln1_g, w_in, w_dw, b_dw, conv_ln_g, conv_ln_b, q_norm_g, k_norm_g, rel_bias,
           out_norm_conv_g, out_norm_attn_g, w_out, ln2_g, w_ff1, w_ff2):
    b, s, d = x.shape
    depth = w_in.shape[0]
    cw = w_dw.shape[2]
    aw = out_norm_attn_g.shape[1]
    ff = w_ff1.shape[2]
    qcol = 2 * cw // aw
    x2 = x.reshape(b * s, d)
    for l in range(depth):
        w_vg = w_in[l, :, :2 * cw].astype(BF16)
        vg, h, (wq, wk, wv, wo) = _vg_proj(
            x2, ln1_g[l].reshape(1, d), w_vg,
            [(w_in[l], aw, qcol), (w_in[l], aw, qcol + 1), (w_in[l], aw, qcol + 2), (w_out[l], d, 0)])
        q, k, v, (w1, w2) = _qkv_proj(h, wq, wk, wv, q_norm_g[l], k_norm_g[l],
                                      [(w_ff1[l], ff, 0), (w_ff2[l], d, 0)])
        yc = _conv_branch(vg, w_dw[l], b_dw[l], conv_ln_g[l], conv_ln_b[l], out_norm_conv_g[l], s)
        bias = _rel_bias(rel_bias[l])
        ya = _attention(q.reshape(b, s, aw), k.reshape(b, s, aw), v.reshape(b, s, aw),
                        bias, out_norm_attn_g[l])
        x2 = _out_proj(x2, yc, ya.reshape(b * s, aw), wo)
        x2 = _ffn(x2, ln2_g[l].reshape(1, d), w1, w2)
    return x2.reshape(b, s, d)
```

```python
import functools
import math

import jax
import jax.numpy as jnp
import numpy as np
from jax import lax
from jax.experimental import pallas as pl
from jax.experimental.pallas import tpu as pltpu

F32 = jnp.float32
BF16 = jnp.bfloat16

EPS = 1e-6
CHUNK = 64
LEFT_CHUNKS = 8
HEAD_DIM = 128
CONV_KERNEL = 31
MAX_REL_DIST = 128
LANES = 128
BF16_ROWS = 16
MASK_VALUE = -1e30
LOG2E = math.log2(math.e)

Q_BLOCK = 256
KV_BLOCKS = 3
Q_PER_STEP = 2
HALO = 32
ROW_TILE = 512
WIDE_TILE = 1024
SUB_ROWS = 256
CONV_ROWS = 64
NORM_ROWS = 64
MXU_COLS = 512
FFN_ROWS = 512
FFN_COLS = 2048
FFN_CHUNK = 512
MIB = 1024 * 1024


def _rms_scale(x):
    return lax.rsqrt(jnp.mean(x * x, axis=-1, keepdims=True) + EPS)


def _resident(shape, index=None):
    index = index or (0,) * len(shape)
    return pl.BlockSpec(shape, lambda *_: index, pipeline_mode=pl.Buffered(1))


def _cast_specs(casts, steps):
    in_specs, out_specs, shapes, args = [], [], [], []
    for arr, width, col in casts:
        slab = arr.shape[0] // steps
        in_specs.append(pl.BlockSpec((slab, width), lambda i, col=col: (i, col)))
        out_specs.append(pl.BlockSpec((slab, width), lambda i: (i, 0)))
        shapes.append(jax.ShapeDtypeStruct((arr.shape[0], width), BF16))
        args.append(arr)
    return in_specs, out_specs, shapes, args


def _vg_proj_kernel(x_ref, g_ref, w_ref, *refs, n_cast):
    cast_in = refs[:n_cast]
    vg_ref, h_ref = refs[n_cast:n_cast + 2]
    cast_out = refs[n_cast + 2:]
    tm = x_ref.shape[0]
    n = w_ref.shape[1]

    def norm_rows(r0, nrows):
        x = x_ref[r0:r0 + nrows, :]
        h_ref[r0:r0 + nrows, :] = (x * _rms_scale(x) * g_ref[...]).astype(BF16)

    for r0 in range(0, tm, NORM_ROWS):
        norm_rows(r0, NORM_ROWS)
    for s in range(tm // SUB_ROWS):
        rows = slice(s * SUB_ROWS, (s + 1) * SUB_ROWS)
        for c0 in range(0, n, MXU_COLS):
            y = jnp.dot(h_ref[rows, :], w_ref[:, c0:c0 + MXU_COLS], preferred_element_type=F32)
            for g in range(MXU_COLS // LANES):
                vg_ref[c0 // LANES + g, rows, :] = y[:, g * LANES:(g + 1) * LANES].astype(vg_ref.dtype)
    for src, dst in zip(cast_in, cast_out):
        dst[...] = src[...].astype(dst.dtype)


def _vg_proj(x2, g, w_vg, casts):
    m, d = x2.shape
    n = w_vg.shape[1]
    steps = m // WIDE_TILE
    cast_in, cast_out, cast_shapes, cast_args = _cast_specs(casts, steps)
    outs = pl.pallas_call(
        functools.partial(_vg_proj_kernel, n_cast=len(casts)),
        grid=(steps,),
        in_specs=[pl.BlockSpec((WIDE_TILE, d), lambda i: (i, 0)),
                  _resident((1, d)), _resident(w_vg.shape)] + cast_in,
        out_specs=[pl.BlockSpec((n // LANES, WIDE_TILE, LANES), lambda i: (0, i, 0)),
                   pl.BlockSpec((WIDE_TILE, d), lambda i: (i, 0))] + cast_out,
        out_shape=[jax.ShapeDtypeStruct((n // LANES, m, LANES), BF16),
                   jax.ShapeDtypeStruct((m, d), BF16)] + cast_shapes,
        compiler_params=pltpu.CompilerParams(
            dimension_semantics=("arbitrary",),
            vmem_limit_bytes=56 * MIB),
        name="vg_proj",
    )(x2, g, w_vg, *cast_args)
    return outs[0], outs[1], outs[2:]


def _qkv_proj_kernel(h_ref, wq_ref, wk_ref, wv_ref, qg_ref, kg_ref, *refs, n_cast):
    cast_in = refs[:n_cast]
    q_ref, k_ref, v_ref = refs[n_cast:n_cast + 3]
    cast_out = refs[n_cast + 3:]
    tm = h_ref.shape[0]
    aw = q_ref.shape[1]

    def head_norm(y, gain, out_ref, rows):
        for hd in range(aw // HEAD_DIM):
            sl = slice(hd * HEAD_DIM, (hd + 1) * HEAD_DIM)
            blk = y[:, sl]
            out_ref[rows, sl] = (blk * _rms_scale(blk) * gain).astype(out_ref.dtype)

    qg = qg_ref[...] * (HEAD_DIM ** -0.5 * LOG2E)
    kg = kg_ref[...]
    for s in range(tm // SUB_ROWS):
        rows = slice(s * SUB_ROWS, (s + 1) * SUB_ROWS)
        h = h_ref[rows, :]
        head_norm(jnp.dot(h, wq_ref[...], preferred_element_type=F32), qg, q_ref, rows)
        head_norm(jnp.dot(h, wk_ref[...], preferred_element_type=F32), kg, k_ref, rows)
        v_ref[rows, :] = jnp.dot(h, wv_ref[...], preferred_element_type=F32).astype(v_ref.dtype)
    for src, dst in zip(cast_in, cast_out):
        dst[...] = src[...].astype(dst.dtype)


def _qkv_proj(h2, wq, wk, wv, q_g, k_g, casts):
    m, d = h2.shape
    aw = wq.shape[1]
    steps = m // ROW_TILE
    row_blk = lambda width: pl.BlockSpec((ROW_TILE, width), lambda i: (i, 0))
    cast_in, cast_out, cast_shapes, cast_args = _cast_specs(casts, steps)
    outs = pl.pallas_call(
        functools.partial(_qkv_proj_kernel, n_cast=len(casts)),
        grid=(steps,),
        in_specs=[row_blk(d), _resident(wq.shape), _resident(wk.shape), _resident(wv.shape),
                  _resident((1, HEAD_DIM)), _resident((1, HEAD_DIM))] + cast_in,
        out_specs=[row_blk(aw)] * 3 + cast_out,
        out_shape=[jax.ShapeDtypeStruct((m, aw), BF16)] * 3 + cast_shapes,
        compiler_params=pltpu.CompilerParams(
            dimension_semantics=("arbitrary",),
            vmem_limit_bytes=48 * MIB),
        name="qkv_proj",
    )(h2, wq, wk, wv, q_g.reshape(1, HEAD_DIM), k_g.reshape(1, HEAD_DIM), *cast_args)
    return outs[0], outs[1], outs[2], outs[3:]


def _conv_kernel(vg_ref, halo_ref, cw_ref, cb_ref, lng_ref, lnb_ref, og_ref, o_ref,
                 hb_ref, hs_ref, y_ref, *, tiles_per_seq):
    groups = y_ref.shape[0]
    ts = y_ref.shape[1]
    width = groups * LANES
    first_tile = pl.program_id(0) % tiles_per_seq == 0
    first = HALO - (CONV_KERNEL - 1)

    def group_body(c, carry):
        halo = halo_ref[c].astype(F32) * jax.nn.sigmoid(halo_ref[groups + c].astype(F32))
        hb_ref[0:HALO, :] = jnp.where(first_tile, 0.0, halo)
        hb_ref[HALO:HALO + ts, :] = vg_ref[c].astype(F32) * jax.nn.sigmoid(vg_ref[groups + c].astype(F32))
        hb_ref[HALO + ts:, :] = jnp.zeros((BF16_ROWS, LANES), F32)
        for p in range(BF16_ROWS):
            hs_ref[p, :, :] = hb_ref[p:p + HALO + ts, :].astype(BF16)
        for r in range(0, ts, CONV_ROWS):
            prods = []
            for t in range(CONV_KERNEL):
                a, p = divmod(first + t, BF16_ROWS)
                lo = r + a * BF16_ROWS
                w = jnp.concatenate([cw_ref[c, t]] * (CONV_ROWS // BF16_ROWS), axis=0)
                prods.append(w.astype(F32) * hs_ref[p, lo:lo + CONV_ROWS, :].astype(F32))
            y_ref[c, r:r + CONV_ROWS, :] = functools.reduce(lambda u, v: u + v, prods) + cb_ref[c]
        return carry

    lax.fori_loop(0, groups, group_body, 0)

    for r in range(0, ts, CONV_ROWS):
        rows = slice(r, r + CONV_ROWS)
        y = y_ref[:, rows, :]
        mu = jnp.sum(jnp.sum(y, axis=0), axis=-1, keepdims=True) / width
        yc = y - mu[None]
        var = jnp.sum(jnp.sum(yc * yc, axis=0), axis=-1, keepdims=True) / width
        z = yc * lax.rsqrt(var + EPS)[None] * lng_ref[...] + lnb_ref[...]
        sw = z * jax.nn.sigmoid(z)
        ms = jnp.sum(jnp.sum(sw * sw, axis=0), axis=-1, keepdims=True) / width
        out = sw * lax.rsqrt(ms + EPS)[None] * og_ref[...]
        for c in range(groups):
            o_ref[rows, c * LANES:(c + 1) * LANES] = out[c].astype(o_ref.dtype)


def _conv_branch(vg3, w_dw, b_dw, ln_g, ln_b, out_g, seq_len):
    n_groups2, m, _ = vg3.shape
    cw = w_dw.shape[1]
    groups = cw // LANES
    ts = ROW_TILE
    w_g = w_dw.reshape(CONV_KERNEL, groups, LANES).transpose(1, 0, 2)
    w_g = jnp.broadcast_to(w_g[:, :, None, :], (groups, CONV_KERNEL, BF16_ROWS, LANES)).astype(BF16)
    as_groups = lambda v: v.reshape(groups, 1, LANES)
    per_halo = ts // HALO
    return pl.pallas_call(
        functools.partial(_conv_kernel, tiles_per_seq=seq_len // ts),
        grid=(m // ts,),
        in_specs=[
            pl.BlockSpec((n_groups2, ts, LANES), lambda i: (0, i, 0)),
            pl.BlockSpec((n_groups2, HALO, LANES), lambda i: (0, jnp.maximum(i * per_halo - 1, 0), 0)),
            _resident((groups, CONV_KERNEL, BF16_ROWS, LANES)),
            _resident((groups, 1, LANES)), _resident((groups, 1, LANES)),
            _resident((groups, 1, LANES)), _resident((groups, 1, LANES)),
        ],
        out_specs=pl.BlockSpec((ts, cw), lambda i: (i, 0)),
        out_shape=jax.ShapeDtypeStruct((m, cw), BF16),
        scratch_shapes=[pltpu.VMEM((HALO + ts + BF16_ROWS, LANES), F32),
                        pltpu.VMEM((BF16_ROWS, HALO + ts, LANES), BF16),
                        pltpu.VMEM((groups, ts, LANES), F32)],
        compiler_params=pltpu.CompilerParams(
            dimension_semantics=("arbitrary",),
            vmem_limit_bytes=32 * MIB),
        name="conv_branch",
    )(vg3, vg3, w_g, as_groups(b_dw), as_groups(ln_g), as_groups(ln_b), as_groups(out_g))


def _rel_bias_kernel(row_ref, mask_ref, o_ref):
    width = row_ref.shape[-1]
    x = jnp.broadcast_to(row_ref[0] * LOG2E, (Q_BLOCK, width))
    x = pltpu.roll(x, 0, 1, stride=1, stride_axis=0)
    bias = x[:, Q_BLOCK:] + mask_ref[...]
    col = lax.broadcasted_iota(jnp.int32, bias.shape, 1)
    for t in range(KV_BLOCKS):
        o_ref[t, 0] = jnp.where(col < (KV_BLOCKS - 1 - t) * Q_BLOCK, MASK_VALUE, bias)


def _band_mask():
    r = np.arange(Q_BLOCK)[:, None] // CHUNK
    c = np.arange(KV_BLOCKS * Q_BLOCK)[None, :] // CHUNK
    visible = (c >= r) & (c <= r + LEFT_CHUNKS)
    return np.where(visible, 0.0, MASK_VALUE).astype(np.float32)


def _rel_bias(rel_table):
    h, n_rel = rel_table.shape
    kw = KV_BLOCKS * Q_BLOCK
    width = Q_BLOCK + kw
    left = kw - MAX_REL_DIST
    right = width - left - n_rel
    row = jnp.pad(rel_table[:, ::-1], ((0, 0), (left, right)), mode="edge")
    row = row.reshape(h, 1, width)
    return pl.pallas_call(
        _rel_bias_kernel,
        grid=(h,),
        in_specs=[pl.BlockSpec((1, 1, width), lambda i: (i, 0, 0)),
                  pl.BlockSpec((Q_BLOCK, kw), lambda i: (0, 0))],
        out_specs=pl.BlockSpec((KV_BLOCKS, 1, Q_BLOCK, kw), lambda i: (0, i, 0, 0)),
        out_shape=jax.ShapeDtypeStruct((KV_BLOCKS, h, Q_BLOCK, kw), F32),
        compiler_params=pltpu.CompilerParams(dimension_semantics=("arbitrary",)),
        name="rel_bias",
    )(row, jnp.asarray(_band_mask()))


def _attn_kernel(q_ref, kp_ref, kc_ref, vp_ref, vc_ref, bias_ref, og_ref, o_ref,
                 acc_ref, vext_ref):
    i = pl.program_id(1)
    n_heads = acc_ref.shape[1] // HEAD_DIM
    nt = (((1,), (1,)), ((), ()))
    win = KV_BLOCKS * Q_BLOCK
    tile = Q_PER_STEP * Q_BLOCK

    @pl.when((pl.program_id(0) == 0) & (i == 0))
    def _():
        vext_ref[:, :, HEAD_DIM:] = jnp.ones(vext_ref.shape[:2] + (HEAD_DIM,), BF16)

    for h in range(n_heads):
        sl = slice(h * HEAD_DIM, (h + 1) * HEAD_DIM)
        vext_ref[h, :tile, :HEAD_DIM] = vp_ref[0, :, sl]
        vext_ref[h, tile:, :HEAD_DIM] = vc_ref[0, :, sl]

    def key_block(n, sl):
        ref = kp_ref if n < Q_PER_STEP else kc_ref
        r0 = (n % Q_PER_STEP) * Q_BLOCK
        return ref[0, r0:r0 + Q_BLOCK, sl]

    for qb in range(Q_PER_STEP):
        rows = slice(qb * Q_BLOCK, (qb + 1) * Q_BLOCK)
        variant = jnp.minimum(i * Q_PER_STEP + qb, KV_BLOCKS - 1)
        for h in range(n_heads):
            sl = slice(h * HEAD_DIM, (h + 1) * HEAD_DIM)
            q = q_ref[0, rows, sl]
            scores = []
            for j in range(KV_BLOCKS):
                s = lax.dot_general(q, key_block(qb + j, sl), nt, preferred_element_type=F32)
                scores.append(s + bias_ref[variant, h, :, j * Q_BLOCK:(j + 1) * Q_BLOCK])
            m = functools.reduce(jnp.maximum, [jnp.max(s, axis=-1, keepdims=True) for s in scores])
            p = jnp.concatenate([jnp.exp2(s - m).astype(BF16) for s in scores], axis=1)
            o = jnp.dot(p, vext_ref[h, qb * Q_BLOCK:qb * Q_BLOCK + win, :],
                        preferred_element_type=F32)
            acc_ref[rows, sl] = o[:, :HEAD_DIM] / o[:, HEAD_DIM:]
        y = acc_ref[rows, :]
        o_ref[0, rows, :] = (y * _rms_scale(y) * og_ref[...]).astype(o_ref.dtype)


def _attention(q3, k3, v3, bias, out_g):
    b, s, aw = q3.shape
    h = bias.shape[1]
    tile = Q_PER_STEP * Q_BLOCK
    blk = (1, tile, aw)
    cur = pl.BlockSpec(blk, lambda bi, i: (bi, i, 0))
    prev = pl.BlockSpec(blk, lambda bi, i: (bi, jnp.maximum(i - 1, 0), 0))
    return pl.pallas_call(
        _attn_kernel,
        grid=(b, s // tile),
        in_specs=[cur, prev, cur, prev, cur, _resident(bias.shape), _resident((1, aw))],
        out_specs=pl.BlockSpec(blk, lambda bi, i: (bi, i, 0)),
        out_shape=jax.ShapeDtypeStruct((b, s, aw), BF16),
        scratch_shapes=[pltpu.VMEM((tile, aw), F32),
                        pltpu.VMEM((h, 2 * tile, 2 * HEAD_DIM), BF16)],
        compiler_params=pltpu.CompilerParams(
            dimension_semantics=("arbitrary", "arbitrary"),
            vmem_limit_bytes=48 * MIB),
        name="attention",
    )(q3, k3, k3, v3, v3, bias, out_g.reshape(1, aw))


def _out_proj_kernel(x_ref, yc_ref, ya_ref, wc_ref, wa_ref, o_ref):
    for r0 in range(0, x_ref.shape[0], ROW_TILE):
        rows = slice(r0, r0 + ROW_TILE)
        acc = jnp.dot(yc_ref[rows, :], wc_ref[...], preferred_element_type=F32)
        acc = acc + jnp.dot(ya_ref[rows, :], wa_ref[...], preferred_element_type=F32)
        o_ref[rows, :] = x_ref[rows, :] + acc


def _out_proj(x2, yc2, ya2, w_out):
    m, d = x2.shape
    cw = yc2.shape[1]
    aw = ya2.shape[1]
    tm = WIDE_TILE
    row_blk = lambda width: pl.BlockSpec((tm, width), lambda i: (i, 0))
    return pl.pallas_call(
        _out_proj_kernel,
        grid=(m // tm,),
        in_specs=[row_blk(d), row_blk(cw), row_blk(aw),
                  _resident((cw, d), (0, 0)), _resident((aw, d), (cw // aw, 0))],
        out_specs=row_blk(d),
        out_shape=jax.ShapeDtypeStruct((m, d), F32),
        compiler_params=pltpu.CompilerParams(
            dimension_semantics=("arbitrary",),
            vmem_limit_bytes=56 * MIB),
        name="out_proj",
    )(x2, yc2, ya2, w_out, w_out)


def _ffn_kernel(x_ref, g_ref, w1_ref, w2_ref, o_ref, h_ref):
    @pl.when(pl.program_id(1) == 0)
    def _():
        for r0 in range(0, x_ref.shape[0], SUB_ROWS):
            x = x_ref[r0:r0 + SUB_ROWS, :]
            h_ref[r0:r0 + SUB_ROWS, :] = (x * _rms_scale(x) * g_ref[...]).astype(BF16)
            o_ref[r0:r0 + SUB_ROWS, :] = x

    for c0 in range(0, w1_ref.shape[1], FFN_CHUNK):
        a = jnp.dot(h_ref[...], w1_ref[:, c0:c0 + FFN_CHUNK], preferred_element_type=F32)
        a = jnp.maximum(a, 0.0)
        a = (a * a).astype(BF16)
        o_ref[...] += jnp.dot(a, w2_ref[c0:c0 + FFN_CHUNK, :], preferred_element_type=F32)


def _ffn(x2, g, w1, w2):
    m, d = x2.shape
    ff = w1.shape[1]
    tm, tf = FFN_ROWS, FFN_COLS
    return pl.pallas_call(
        _ffn_kernel,
        grid=(m // tm, ff // tf),
        in_specs=[
            pl.BlockSpec((tm, d), lambda i, f: (i, 0)),
            _resident((1, d)),
            pl.BlockSpec((d, tf), lambda i, f: (0, f)),
            pl.BlockSpec((tf, d), lambda i, f: (f, 0)),
        ],
        out_specs=pl.BlockSpec((tm, d), lambda i, f: (i, 0)),
        out_shape=jax.ShapeDtypeStruct((m, d), F32),
        scratch_shapes=[pltpu.VMEM((tm, d), BF16)],
        compiler_params=pltpu.CompilerParams(
            dimension_semantics=("arbitrary", "arbitrary"),
            vmem_limit_bytes=56 * MIB),
        name="ffn",
    )(x2, g, w1, w2)


def kernel(x, ln1_g, w_in, w_dw, b_dw, conv_ln_g, conv_ln_b, q_norm_g, k_norm_g, rel_bias,
           out_norm_conv_g, out_norm_attn_g, w_out, ln2_g, w_ff1, w_ff2):
    b, s, d = x.shape
    depth = w_in.shape[0]
    cw = w_dw.shape[2]
    aw = out_norm_attn_g.shape[1]
    ff = w_ff1.shape[2]
    qcol = 2 * cw // aw
    x2 = x.reshape(b * s, d)
    for l in range(depth):
        w_vg = w_in[l, :, :2 * cw].astype(BF16)
        vg, h, (wq, wk, wv, wo) = _vg_proj(
            x2, ln1_g[l].reshape(1, d), w_vg,
            [(w_in[l], aw, qcol), (w_in[l], aw, qcol + 1), (w_in[l], aw, qcol + 2), (w_out[l], d, 0)])
        q, k, v, (w1, w2) = _qkv_proj(h, wq, wk, wv, q_norm_g[l], k_norm_g[l],
                                      [(w_ff1[l], ff, 0), (w_ff2[l], d, 0)])
        yc = _conv_branch(vg, w_dw[l], b_dw[l], conv_ln_g[l], conv_ln_b[l], out_norm_conv_g[l], s)
        bias = _rel_bias(rel_bias[l])
        ya = _attention(q.reshape(b, s, aw), k.reshape(b, s, aw), v.reshape(b, s, aw),
                        bias, out_norm_attn_g[l])
        x2 = _out_proj(x2, yc, ya.reshape(b * s, aw), wo)
        x2 = _ffn(x2, ln2_g[l].reshape(1, d), w1, w2)
    return x2.reshape(b, s, d)
```

```python
import functools
import math

import jax
import jax.numpy as jnp
import numpy as np
from jax import lax
from jax.experimental import pallas as pl
from jax.experimental.pallas import tpu as pltpu

F32 = jnp.float32
BF16 = jnp.bfloat16

EPS = 1e-6
CHUNK = 64
LEFT_CHUNKS = 8
HEAD_DIM = 128
CONV_KERNEL = 31
MAX_REL_DIST = 128
LANES = 128
BF16_ROWS = 16
MASK_VALUE = -1e30
LOG2E = math.log2(math.e)

Q_BLOCK = 256
KV_BLOCKS = 3
Q_PER_STEP = 2
HALO = 32
ROW_TILE = 512
WIDE_TILE = 1024
SUB_ROWS = 256
CONV_ROWS = 64
NORM_ROWS = 64
MXU_COLS = 512
FFN_ROWS = 512
FFN_COLS = 2048
FFN_CHUNK = 512
MIB = 1024 * 1024


def _rms_scale(x):
    return lax.rsqrt(jnp.mean(x * x, axis=-1, keepdims=True) + EPS)


def _resident(shape, index=None):
    index = index or (0,) * len(shape)
    return pl.BlockSpec(shape, lambda *_: index, pipeline_mode=pl.Buffered(1))


def _cast_specs(casts, steps):
    in_specs, out_specs, shapes, args = [], [], [], []
    for arr, width, col in casts:
        slab = arr.shape[0] // steps
        in_specs.append(pl.BlockSpec((slab, width), lambda i, col=col: (i, col)))
        out_specs.append(pl.BlockSpec((slab, width), lambda i: (i, 0)))
        shapes.append(jax.ShapeDtypeStruct((arr.shape[0], width), BF16))
        args.append(arr)
    return in_specs, out_specs, shapes, args


def _vg_proj_kernel(x_ref, g_ref, w_ref, *refs, n_cast):
    cast_in = refs[:n_cast]
    vg_ref, h_ref = refs[n_cast:n_cast + 2]
    cast_out = refs[n_cast + 2:]
    tm = x_ref.shape[0]
    n = w_ref.shape[1]

    def norm_rows(r0, nrows):
        x = x_ref[r0:r0 + nrows, :]
        h_ref[r0:r0 + nrows, :] = (x * _rms_scale(x) * g_ref[...]).astype(BF16)

    for r0 in range(0, tm, NORM_ROWS):
        norm_rows(r0, NORM_ROWS)
    for s in range(tm // SUB_ROWS):
        rows = slice(s * SUB_ROWS, (s + 1) * SUB_ROWS)
        for c0 in range(0, n, MXU_COLS):
            y = jnp.dot(h_ref[rows, :], w_ref[:, c0:c0 + MXU_COLS], preferred_element_type=F32)
            for g in range(MXU_COLS // LANES):
                vg_ref[c0 // LANES + g, rows, :] = y[:, g * LANES:(g + 1) * LANES].astype(vg_ref.dtype)
    for src, dst in zip(cast_in, cast_out):
        dst[...] = src[...].astype(dst.dtype)


def _vg_proj(x2, g, w_vg, casts):
    m, d = x2.shape
    n = w_vg.shape[1]
    steps = m // WIDE_TILE
    cast_in, cast_out, cast_shapes, cast_args = _cast_specs(casts, steps)
    outs = pl.pallas_call(
        functools.partial(_vg_proj_kernel, n_cast=len(casts)),
        grid=(steps,),
        in_specs=[pl.BlockSpec((WIDE_TILE, d), lambda i: (i, 0)),
                  _resident((1, d)), _resident(w_vg.shape)] + cast_in,
        out_specs=[pl.BlockSpec((n // LANES, WIDE_TILE, LANES), lambda i: (0, i, 0)),
                   pl.BlockSpec((WIDE_TILE, d), lambda i: (i, 0))] + cast_out,
        out_shape=[jax.ShapeDtypeStruct((n // LANES, m, LANES), BF16),
                   jax.ShapeDtypeStruct((m, d), BF16)] + cast_shapes,
        compiler_params=pltpu.CompilerParams(
            dimension_semantics=("arbitrary",),
            vmem_limit_bytes=56 * MIB),
        name="vg_proj",
    )(x2, g, w_vg, *cast_args)
    return outs[0], outs[1], outs[2:]


def _qkv_proj_kernel(h_ref, wq_ref, wk_ref, wv_ref, qg_ref, kg_ref, *refs, n_cast):
    cast_in = refs[:n_cast]
    q_ref, k_ref, v_ref = refs[n_cast:n_cast + 3]
    cast_out = refs[n_cast + 3:]
    tm = h_ref.shape[0]
    aw = q_ref.shape[1]

    def head_norm(y, gain, out_ref, rows):
        for hd in range(aw // HEAD_DIM):
            sl = slice(hd * HEAD_DIM, (hd + 1) * HEAD_DIM)
            blk = y[:, sl]
            out_ref[rows, sl] = (blk * _rms_scale(blk) * gain).astype(out_ref.dtype)

    qg = qg_ref[...] * (HEAD_DIM ** -0.5 * LOG2E)
    kg = kg_ref[...]
    for s in range(tm // SUB_ROWS):
        rows = slice(s * SUB_ROWS, (s + 1) * SUB_ROWS)
        h = h_ref[rows, :]
        head_norm(jnp.dot(h, wq_ref[...], preferred_element_type=F32), qg, q_ref, rows)
        head_norm(jnp.dot(h, wk_ref[...], preferred_element_type=F32), kg, k_ref, rows)
        v_ref[rows, :] = jnp.dot(h, wv_ref[...], preferred_element_type=F32).astype(v_ref.dtype)
    for src, dst in zip(cast_in, cast_out):
        dst[...] = src[...].astype(dst.dtype)


def _qkv_proj(h2, wq, wk, wv, q_g, k_g, casts):
    m, d = h2.shape
    aw = wq.shape[1]
    steps = m // ROW_TILE
    row_blk = lambda width: pl.BlockSpec((ROW_TILE, width), lambda i: (i, 0))
    cast_in, cast_out, cast_shapes, cast_args = _cast_specs(casts, steps)
    outs = pl.pallas_call(
        functools.partial(_qkv_proj_kernel, n_cast=len(casts)),
        grid=(steps,),
        in_specs=[row_blk(d), _resident(wq.shape), _resident(wk.shape), _resident(wv.shape),
                  _resident((1, HEAD_DIM)), _resident((1, HEAD_DIM))] + cast_in,
        out_specs=[row_blk(aw)] * 3 + cast_out,
        out_shape=[jax.ShapeDtypeStruct((m, aw), BF16)] * 3 + cast_shapes,
        compiler_params=pltpu.CompilerParams(
            dimension_semantics=("arbitrary",),
            vmem_limit_bytes=48 * MIB),
        name="qkv_proj",
    )(h2, wq, wk, wv, q_g.reshape(1, HEAD_DIM), k_g.reshape(1, HEAD_DIM), *cast_args)
    return outs[0], outs[1], outs[2], outs[3:]


def _conv_kernel(vg_ref, halo_ref, cw_ref, cb_ref, lng_ref, lnb_ref, og_ref, o_ref,
                 hb_ref, hs_ref, y_ref, *, tiles_per_seq):
    groups = y_ref.shape[0]
    ts = y_ref.shape[1]
    width = groups * LANES
    first_tile = pl.program_id(0) % tiles_per_seq == 0
    first = HALO - (CONV_KERNEL - 1)

    def group_body(c, carry):
        halo = halo_ref[c].astype(F32) * jax.nn.sigmoid(halo_ref[groups + c].astype(F32))
        hb_ref[0:HALO, :] = jnp.where(first_tile, 0.0, halo)
        hb_ref[HALO:HALO + ts, :] = vg_ref[c].astype(F32) * jax.nn.sigmoid(vg_ref[groups + c].astype(F32))
        hb_ref[HALO + ts:, :] = jnp.zeros((BF16_ROWS, LANES), F32)
        for p in range(BF16_ROWS):
            hs_ref[p, :, :] = hb_ref[p:p + HALO + ts, :].astype(BF16)
        for r in range(0, ts, CONV_ROWS):
            prods = []
            for t in range(CONV_KERNEL):
                a, p = divmod(first + t, BF16_ROWS)
                lo = r + a * BF16_ROWS
                prods.append(cw_ref[c, t].astype(F32) * hs_ref[p, lo:lo + CONV_ROWS, :].astype(F32))
            y_ref[c, r:r + CONV_ROWS, :] = functools.reduce(lambda u, v: u + v, prods) + cb_ref[c]
        return carry

    lax.fori_loop(0, groups, group_body, 0)

    for r in range(0, ts, CONV_ROWS):
        rows = slice(r, r + CONV_ROWS)
        y = y_ref[:, rows, :]
        mu = jnp.sum(jnp.sum(y, axis=0), axis=-1, keepdims=True) / width
        yc = y - mu[None]
        var = jnp.sum(jnp.sum(yc * yc, axis=0), axis=-1, keepdims=True) / width
        z = yc * lax.rsqrt(var + EPS)[None] * lng_ref[...] + lnb_ref[...]
        sw = z * jax.nn.sigmoid(z)
        ms = jnp.sum(jnp.sum(sw * sw, axis=0), axis=-1, keepdims=True) / width
        out = sw * lax.rsqrt(ms + EPS)[None] * og_ref[...]
        for c in range(groups):
            o_ref[rows, c * LANES:(c + 1) * LANES] = out[c].astype(o_ref.dtype)


def _conv_branch(vg3, w_dw, b_dw, ln_g, ln_b, out_g, seq_len):
    n_groups2, m, _ = vg3.shape
    cw = w_dw.shape[1]
    groups = cw // LANES
    ts = ROW_TILE
    w_g = w_dw.reshape(CONV_KERNEL, groups, LANES).transpose(1, 0, 2)
    w_g = jnp.broadcast_to(w_g[:, :, None, :], (groups, CONV_KERNEL, CONV_ROWS, LANES)).astype(BF16)
    as_groups = lambda v: v.reshape(groups, 1, LANES)
    per_halo = ts // HALO
    return pl.pallas_call(
        functools.partial(_conv_kernel, tiles_per_seq=seq_len // ts),
        grid=(m // ts,),
        in_specs=[
            pl.BlockSpec((n_groups2, ts, LANES), lambda i: (0, i, 0)),
            pl.BlockSpec((n_groups2, HALO, LANES), lambda i: (0, jnp.maximum(i * per_halo - 1, 0), 0)),
            _resident((groups, CONV_KERNEL, CONV_ROWS, LANES)),
            _resident((groups, 1, LANES)), _resident((groups, 1, LANES)),
            _resident((groups, 1, LANES)), _resident((groups, 1, LANES)),
        ],
        out_specs=pl.BlockSpec((ts, cw), lambda i: (i, 0)),
        out_shape=jax.ShapeDtypeStruct((m, cw), BF16),
        scratch_shapes=[pltpu.VMEM((HALO + ts + BF16_ROWS, LANES), F32),
                        pltpu.VMEM((BF16_ROWS, HALO + ts, LANES), BF16),
                        pltpu.VMEM((groups, ts, LANES), F32)],
        compiler_params=pltpu.CompilerParams(
            dimension_semantics=("arbitrary",),
            vmem_limit_bytes=32 * MIB),
        name="conv_branch",
    )(vg3, vg3, w_g, as_groups(b_dw), as_groups(ln_g), as_groups(ln_b), as_groups(out_g))


def _rel_bias_kernel(row_ref, mask_ref, w_ref, o_ref, w_bf_ref):
    width = row_ref.shape[-1]
    x = jnp.broadcast_to(row_ref[0] * LOG2E, (Q_BLOCK, width))
    x = pltpu.roll(x, 0, 1, stride=1, stride_axis=0)
    bias = x[:, Q_BLOCK:] + mask_ref[...]
    col = lax.broadcasted_iota(jnp.int32, bias.shape, 1)
    for t in range(KV_BLOCKS):
        o_ref[t, 0] = jnp.where(col < (KV_BLOCKS - 1 - t) * Q_BLOCK, MASK_VALUE, bias)
    w_bf_ref[...] = w_ref[...].astype(w_bf_ref.dtype)


def _band_mask():
    r = np.arange(Q_BLOCK)[:, None] // CHUNK
    c = np.arange(KV_BLOCKS * Q_BLOCK)[None, :] // CHUNK
    visible = (c >= r) & (c <= r + LEFT_CHUNKS)
    return np.where(visible, 0.0, MASK_VALUE).astype(np.float32)


def _rel_bias(rel_table, cast):
    h, n_rel = rel_table.shape
    kw = KV_BLOCKS * Q_BLOCK
    width = Q_BLOCK + kw
    left = kw - MAX_REL_DIST
    right = width - left - n_rel
    row = jnp.pad(rel_table[:, ::-1], ((0, 0), (left, right)), mode="edge")
    row = row.reshape(h, 1, width)
    cast_in, cast_out, cast_shapes, cast_args = _cast_specs([cast], h)
    return pl.pallas_call(
        _rel_bias_kernel,
        grid=(h,),
        in_specs=[pl.BlockSpec((1, 1, width), lambda i: (i, 0, 0)),
                  pl.BlockSpec((Q_BLOCK, kw), lambda i: (0, 0))] + cast_in,
        out_specs=[pl.BlockSpec((KV_BLOCKS, 1, Q_BLOCK, kw), lambda i: (0, i, 0, 0))] + cast_out,
        out_shape=[jax.ShapeDtypeStruct((KV_BLOCKS, h, Q_BLOCK, kw), F32)] + cast_shapes,
        compiler_params=pltpu.CompilerParams(dimension_semantics=("arbitrary",)),
        name="rel_bias",
    )(row, jnp.asarray(_band_mask()), *cast_args)


def _attn_kernel(q_ref, kp_ref, kc_ref, vp_ref, vc_ref, bias_ref, og_ref, o_ref,
                 acc_ref, vext_ref):
    i = pl.program_id(1)
    n_heads = acc_ref.shape[1] // HEAD_DIM
    nt = (((1,), (1,)), ((), ()))
    win = KV_BLOCKS * Q_BLOCK
    tile = Q_PER_STEP * Q_BLOCK

    @pl.when((pl.program_id(0) == 0) & (i == 0))
    def _():
        vext_ref[:, :, HEAD_DIM:] = jnp.ones(vext_ref.shape[:2] + (HEAD_DIM,), BF16)

    for h in range(n_heads):
        sl = slice(h * HEAD_DIM, (h + 1) * HEAD_DIM)
        vext_ref[h, :tile, :HEAD_DIM] = vp_ref[0, :, sl]
        vext_ref[h, tile:, :HEAD_DIM] = vc_ref[0, :, sl]

    def key_block(n, sl):
        ref = kp_ref if n < Q_PER_STEP else kc_ref
        r0 = (n % Q_PER_STEP) * Q_BLOCK
        return ref[0, r0:r0 + Q_BLOCK, sl]

    for qb in range(Q_PER_STEP):
        rows = slice(qb * Q_BLOCK, (qb + 1) * Q_BLOCK)
        variant = jnp.minimum(i * Q_PER_STEP + qb, KV_BLOCKS - 1)
        for h in range(n_heads):
            sl = slice(h * HEAD_DIM, (h + 1) * HEAD_DIM)
            q = q_ref[0, rows, sl]
            scores = []
            for j in range(KV_BLOCKS):
                s = lax.dot_general(q, key_block(qb + j, sl), nt, preferred_element_type=F32)
                scores.append(s + bias_ref[variant, h, :, j * Q_BLOCK:(j + 1) * Q_BLOCK])
            m = functools.reduce(jnp.maximum, [jnp.max(s, axis=-1, keepdims=True) for s in scores])
            p = jnp.concatenate([jnp.exp2(s - m).astype(BF16) for s in scores], axis=1)
            o = jnp.dot(p, vext_ref[h, qb * Q_BLOCK:qb * Q_BLOCK + win, :],
                        preferred_element_type=F32)
            acc_ref[rows, sl] = o[:, :HEAD_DIM] / o[:, HEAD_DIM:]
        y = acc_ref[rows, :]
        o_ref[0, rows, :] = (y * _rms_scale(y) * og_ref[...]).astype(o_ref.dtype)


def _attention(q3, k3, v3, bias, out_g):
    b, s, aw = q3.shape
    h = bias.shape[1]
    tile = Q_PER_STEP * Q_BLOCK
    blk = (1, tile, aw)
    cur = pl.BlockSpec(blk, lambda bi, i: (bi, i, 0))
    prev = pl.BlockSpec(blk, lambda bi, i: (bi, jnp.maximum(i - 1, 0), 0))
    return pl.pallas_call(
        _attn_kernel,
        grid=(b, s // tile),
        in_specs=[cur, prev, cur, prev, cur, _resident(bias.shape), _resident((1, aw))],
        out_specs=pl.BlockSpec(blk, lambda bi, i: (bi, i, 0)),
        out_shape=jax.ShapeDtypeStruct((b, s, aw), BF16),
        scratch_shapes=[pltpu.VMEM((tile, aw), F32),
                        pltpu.VMEM((h, 2 * tile, 2 * HEAD_DIM), BF16)],
        compiler_params=pltpu.CompilerParams(
            dimension_semantics=("arbitrary", "arbitrary"),
            vmem_limit_bytes=48 * MIB),
        name="attention",
    )(q3, k3, k3, v3, v3, bias, out_g.reshape(1, aw))


def _out_proj_kernel(x_ref, yc_ref, ya_ref, wc_ref, wa_ref, o_ref):
    for r0 in range(0, x_ref.shape[0], ROW_TILE):
        rows = slice(r0, r0 + ROW_TILE)
        acc = jnp.dot(yc_ref[rows, :], wc_ref[...], preferred_element_type=F32)
        acc = acc + jnp.dot(ya_ref[rows, :], wa_ref[...], preferred_element_type=F32)
        o_ref[rows, :] = x_ref[rows, :] + acc


def _out_proj(x2, yc2, ya2, w_out):
    m, d = x2.shape
    cw = yc2.shape[1]
    aw = ya2.shape[1]
    tm = WIDE_TILE
    row_blk = lambda width: pl.BlockSpec((tm, width), lambda i: (i, 0))
    return pl.pallas_call(
        _out_proj_kernel,
        grid=(m // tm,),
        in_specs=[row_blk(d), row_blk(cw), row_blk(aw),
                  _resident((cw, d), (0, 0)), _resident((aw, d), (cw // aw, 0))],
        out_specs=row_blk(d),
        out_shape=jax.ShapeDtypeStruct((m, d), F32),
        compiler_params=pltpu.CompilerParams(
            dimension_semantics=("arbitrary",),
            vmem_limit_bytes=56 * MIB),
        name="out_proj",
    )(x2, yc2, ya2, w_out, w_out)


def _ffn_kernel(x_ref, g_ref, w1_ref, w2_ref, o_ref, h_ref):
    @pl.when(pl.program_id(1) == 0)
    def _():
        for r0 in range(0, x_ref.shape[0], SUB_ROWS):
            x = x_ref[r0:r0 + SUB_ROWS, :]
            h_ref[r0:r0 + SUB_ROWS, :] = (x * _rms_scale(x) * g_ref[...]).astype(BF16)
            o_ref[r0:r0 + SUB_ROWS, :] = x

    for c0 in range(0, w1_ref.shape[1], FFN_CHUNK):
        a = jnp.dot(h_ref[...], w1_ref[:, c0:c0 + FFN_CHUNK], preferred_element_type=F32)
        a = jnp.maximum(a, 0.0)
        a = (a * a).astype(BF16)
        o_ref[...] += jnp.dot(a, w2_ref[c0:c0 + FFN_CHUNK, :], preferred_element_type=F32)


def _ffn(x2, g, w1, w2):
    m, d = x2.shape
    ff = w1.shape[1]
    tm, tf = FFN_ROWS, FFN_COLS
    return pl.pallas_call(
        _ffn_kernel,
        grid=(m // tm, ff // tf),
        in_specs=[
            pl.BlockSpec((tm, d), lambda i, f: (i, 0)),
            _resident((1, d)),
            pl.BlockSpec((d, tf), lambda i, f: (0, f)),
            pl.BlockSpec((tf, d), lambda i, f: (f, 0)),
        ],
        out_specs=pl.BlockSpec((tm, d), lambda i, f: (i, 0)),
        out_shape=jax.ShapeDtypeStruct((m, d), F32),
        scratch_shapes=[pltpu.VMEM((tm, d), BF16)],
        compiler_params=pltpu.CompilerParams(
            dimension_semantics=("arbitrary", "arbitrary"),
            vmem_limit_bytes=56 * MIB),
        name="ffn",
    )(x2, g, w1, w2)


def kernel(x, ln1_g, w_in, w_dw, b_dw, conv_ln_g, conv_ln_b, q_norm_g, k_norm_g, rel_bias,
           out_norm_conv_g, out_norm_attn_g, w_out, ln2_g, w_ff1, w_ff2):
    b, s, d = x.shape
    depth = w_in.shape[0]
    cw = w_dw.shape[2]
    aw = out_norm_attn_g.shape[1]
    ff = w_ff1.shape[2]
    qcol = 2 * cw // aw
    x2 = x.reshape(b * s, d)
    for l in range(depth):
        bias, w_vg = _rel_bias(rel_bias[l], (w_in[l], 2 * cw, 0))
        vg, h, (wq, wk, wv, wo) = _vg_proj(
            x2, ln1_g[l].reshape(1, d), w_vg,
            [(w_in[l], aw, qcol), (w_in[l], aw, qcol + 1), (w_in[l], aw, qcol + 2), (w_out[l], d, 0)])
        q, k, v, (w1, w2) = _qkv_proj(h, wq, wk, wv, q_norm_g[l], k_norm_g[l],
                                      [(w_ff1[l], ff, 0), (w_ff2[l], d, 0)])
        yc = _conv_branch(vg, w_dw[l], b_dw[l], conv_ln_g[l], conv_ln_b[l], out_norm_conv_g[l], s)
        ya = _attention(q.reshape(b, s, aw), k.reshape(b, s, aw), v.reshape(b, s, aw),
                        bias, out_norm_attn_g[l])
        x2 = _out_proj(x2, yc, ya.reshape(b * s, aw), wo)
        x2 = _ffn(x2, ln2_g[l].reshape(1, d), w1, w2)
    return x2.reshape(b, s, d)
```

```python
import functools
import math

import jax
import jax.numpy as jnp
import numpy as np
from jax import lax
from jax.experimental import pallas as pl
from jax.experimental.pallas import tpu as pltpu

F32 = jnp.float32
BF16 = jnp.bfloat16

EPS = 1e-6
CHUNK = 64
LEFT_CHUNKS = 8
HEAD_DIM = 128
CONV_KERNEL = 31
MAX_REL_DIST = 128
LANES = 128
BF16_ROWS = 16
MASK_VALUE = -1e30
LOG2E = math.log2(math.e)

Q_BLOCK = 256
KV_BLOCKS = 3
Q_PER_STEP = 2
HALO = 32
ROW_TILE = 512
WIDE_TILE = 1024
SUB_ROWS = 256
CONV_ROWS = 32
LN_ROWS = 32
NORM_ROWS = 64
MXU_COLS = 512
FFN_ROWS = 512
FFN_COLS = 2048
FFN_CHUNK = 512
MIB = 1024 * 1024


def _rms_scale(x):
    return lax.rsqrt(jnp.mean(x * x, axis=-1, keepdims=True) + EPS)


def _resident(shape, index=None):
    index = index or (0,) * len(shape)
    return pl.BlockSpec(shape, lambda *_: index, pipeline_mode=pl.Buffered(1))


def _cast_specs(casts, steps):
    in_specs, out_specs, shapes, args = [], [], [], []
    for arr, width, col in casts:
        assert arr.shape[0] % steps == 0 and (col + 1) * width <= arr.shape[1]
        slab = arr.shape[0] // steps
        in_specs.append(pl.BlockSpec((slab, width), lambda i, col=col: (i, col)))
        out_specs.append(pl.BlockSpec((slab, width), lambda i: (i, 0)))
        shapes.append(jax.ShapeDtypeStruct((arr.shape[0], width), BF16))
        args.append(arr)
    return in_specs, out_specs, shapes, args


def _vg_proj_kernel(x_ref, g_ref, w_ref, *refs, n_cast):
    cast_in = refs[:n_cast]
    vg_ref, h_ref = refs[n_cast:n_cast + 2]
    cast_out = refs[n_cast + 2:]
    tm = x_ref.shape[0]
    n = w_ref.shape[1]

    def norm_rows(r0, nrows):
        x = x_ref[r0:r0 + nrows, :]
        h_ref[r0:r0 + nrows, :] = (x * _rms_scale(x) * g_ref[...]).astype(BF16)

    for r0 in range(0, tm, NORM_ROWS):
        norm_rows(r0, NORM_ROWS)
    for s in range(tm // SUB_ROWS):
        rows = slice(s * SUB_ROWS, (s + 1) * SUB_ROWS)
        for c0 in range(0, n, MXU_COLS):
            y = jnp.dot(h_ref[rows, :], w_ref[:, c0:c0 + MXU_COLS], preferred_element_type=F32)
            for g in range(MXU_COLS // LANES):
                vg_ref[c0 // LANES + g, rows, :] = y[:, g * LANES:(g + 1) * LANES].astype(vg_ref.dtype)
    for src, dst in zip(cast_in, cast_out):
        dst[...] = src[...].astype(dst.dtype)


def _vg_proj(x2, g, w_vg, casts):
    m, d = x2.shape
    n = w_vg.shape[1]
    steps = m // WIDE_TILE
    cast_in, cast_out, cast_shapes, cast_args = _cast_specs(casts, steps)
    outs = pl.pallas_call(
        functools.partial(_vg_proj_kernel, n_cast=len(casts)),
        grid=(steps,),
        in_specs=[pl.BlockSpec((WIDE_TILE, d), lambda i: (i, 0)),
                  _resident((1, d)), _resident(w_vg.shape)] + cast_in,
        out_specs=[pl.BlockSpec((n // LANES, WIDE_TILE, LANES), lambda i: (0, i, 0)),
                   pl.BlockSpec((WIDE_TILE, d), lambda i: (i, 0))] + cast_out,
        out_shape=[jax.ShapeDtypeStruct((n // LANES, m, LANES), BF16),
                   jax.ShapeDtypeStruct((m, d), BF16)] + cast_shapes,
        compiler_params=pltpu.CompilerParams(
            dimension_semantics=("arbitrary",),
            vmem_limit_bytes=56 * MIB),
        name="vg_proj",
    )(x2, g, w_vg, *cast_args)
    return outs[0], outs[1], outs[2:]


def _qkv_proj_kernel(h_ref, wq_ref, wk_ref, wv_ref, qg_ref, kg_ref, *refs, n_cast):
    cast_in = refs[:n_cast]
    q_ref, k_ref, v_ref = refs[n_cast:n_cast + 3]
    cast_out = refs[n_cast + 3:]
    tm = h_ref.shape[0]
    aw = q_ref.shape[1]

    def head_norm(y, gain, out_ref, rows):
        for hd in range(aw // HEAD_DIM):
            sl = slice(hd * HEAD_DIM, (hd + 1) * HEAD_DIM)
            blk = y[:, sl]
            out_ref[rows, sl] = (blk * _rms_scale(blk) * gain).astype(out_ref.dtype)

    qg = qg_ref[...] * (HEAD_DIM ** -0.5 * LOG2E)
    kg = kg_ref[...]
    for s in range(tm // SUB_ROWS):
        rows = slice(s * SUB_ROWS, (s + 1) * SUB_ROWS)
        h = h_ref[rows, :]
        head_norm(jnp.dot(h, wq_ref[...], preferred_element_type=F32), qg, q_ref, rows)
        head_norm(jnp.dot(h, wk_ref[...], preferred_element_type=F32), kg, k_ref, rows)
        v_ref[rows, :] = jnp.dot(h, wv_ref[...], preferred_element_type=F32).astype(v_ref.dtype)
    for src, dst in zip(cast_in, cast_out):
        dst[...] = src[...].astype(dst.dtype)


def _qkv_proj(h2, wq, wk, wv, q_g, k_g, casts):
    m, d = h2.shape
    aw = wq.shape[1]
    steps = m // ROW_TILE
    row_blk = lambda width: pl.BlockSpec((ROW_TILE, width), lambda i: (i, 0))
    cast_in, cast_out, cast_shapes, cast_args = _cast_specs(casts, steps)
    outs = pl.pallas_call(
        functools.partial(_qkv_proj_kernel, n_cast=len(casts)),
        grid=(steps,),
        in_specs=[row_blk(d), _resident(wq.shape), _resident(wk.shape), _resident(wv.shape),
                  _resident((1, HEAD_DIM)), _resident((1, HEAD_DIM))] + cast_in,
        out_specs=[row_blk(aw)] * 3 + cast_out,
        out_shape=[jax.ShapeDtypeStruct((m, aw), BF16)] * 3 + cast_shapes,
        compiler_params=pltpu.CompilerParams(
            dimension_semantics=("arbitrary",),
            vmem_limit_bytes=48 * MIB),
        name="qkv_proj",
    )(h2, wq, wk, wv, q_g.reshape(1, HEAD_DIM), k_g.reshape(1, HEAD_DIM), *cast_args)
    return outs[0], outs[1], outs[2], outs[3:]


def _conv_kernel(vg_ref, halo_ref, cw_ref, cb_ref, lng_ref, lnb_ref, og_ref, o_ref,
                 hb_ref, hs_ref, y_ref, *, tiles_per_seq):
    groups = y_ref.shape[0]
    ts = y_ref.shape[1]
    width = groups * LANES
    first_tile = pl.program_id(0) % tiles_per_seq == 0
    first = HALO - (CONV_KERNEL - 1)

    def group_body(c, carry):
        halo = halo_ref[c].astype(F32) * jax.nn.sigmoid(halo_ref[groups + c].astype(F32))
        hb_ref[0:HALO, :] = jnp.where(first_tile, 0.0, halo)
        hb_ref[HALO:HALO + ts, :] = vg_ref[c].astype(F32) * jax.nn.sigmoid(vg_ref[groups + c].astype(F32))
        hb_ref[HALO + ts:, :] = jnp.zeros((BF16_ROWS, LANES), F32)
        for p in range(BF16_ROWS):
            hs_ref[p, :, :] = hb_ref[p:p + HALO + ts, :].astype(BF16)
        for r in range(0, ts, CONV_ROWS):
            prods = []
            for t in range(CONV_KERNEL):
                a, p = divmod(first + t, BF16_ROWS)
                lo = r + a * BF16_ROWS
                prods.append(cw_ref[c, t].astype(F32) * hs_ref[p, lo:lo + CONV_ROWS, :].astype(F32))
            y_ref[c, r:r + CONV_ROWS, :] = functools.reduce(lambda u, v: u + v, prods) + cb_ref[c]
        return carry

    lax.fori_loop(0, groups, group_body, 0)

    for r in range(0, ts, LN_ROWS):
        rows = slice(r, r + LN_ROWS)
        y = y_ref[:, rows, :]
        mu = jnp.sum(jnp.sum(y, axis=0), axis=-1, keepdims=True) / width
        yc = y - mu[None]
        var = jnp.sum(jnp.sum(yc * yc, axis=0), axis=-1, keepdims=True) / width
        z = yc * lax.rsqrt(var + EPS)[None] * lng_ref[...] + lnb_ref[...]
        sw = z * jax.nn.sigmoid(z)
        ms = jnp.sum(jnp.sum(sw * sw, axis=0), axis=-1, keepdims=True) / width
        out = sw * lax.rsqrt(ms + EPS)[None] * og_ref[...]
        for c in range(groups):
            o_ref[rows, c * LANES:(c + 1) * LANES] = out[c].astype(o_ref.dtype)


def _conv_branch(vg3, w_dw, b_dw, ln_g, ln_b, out_g, seq_len):
    n_groups2, m, _ = vg3.shape
    cw = w_dw.shape[1]
    groups = cw // LANES
    ts = ROW_TILE
    w_g = w_dw.reshape(CONV_KERNEL, groups, LANES).transpose(1, 0, 2)
    w_g = jnp.broadcast_to(w_g[:, :, None, :], (groups, CONV_KERNEL, CONV_ROWS, LANES)).astype(BF16)
    as_groups = lambda v: v.reshape(groups, 1, LANES)
    per_halo = ts // HALO
    return pl.pallas_call(
        functools.partial(_conv_kernel, tiles_per_seq=seq_len // ts),
        grid=(m // ts,),
        in_specs=[
            pl.BlockSpec((n_groups2, ts, LANES), lambda i: (0, i, 0)),
            pl.BlockSpec((n_groups2, HALO, LANES), lambda i: (0, jnp.maximum(i * per_halo - 1, 0), 0)),
            _resident((groups, CONV_KERNEL, CONV_ROWS, LANES)),
            _resident((groups, 1, LANES)), _resident((groups, 1, LANES)),
            _resident((groups, 1, LANES)), _resident((groups, 1, LANES)),
        ],
        out_specs=pl.BlockSpec((ts, cw), lambda i: (i, 0)),
        out_shape=jax.ShapeDtypeStruct((m, cw), BF16),
        scratch_shapes=[pltpu.VMEM((HALO + ts + BF16_ROWS, LANES), F32),
                        pltpu.VMEM((BF16_ROWS, HALO + ts, LANES), BF16),
                        pltpu.VMEM((groups, ts, LANES), F32)],
        compiler_params=pltpu.CompilerParams(
            dimension_semantics=("arbitrary",),
            vmem_limit_bytes=32 * MIB),
        name="conv_branch",
    )(vg3, vg3, w_g, as_groups(b_dw), as_groups(ln_g), as_groups(ln_b), as_groups(out_g))


def _rel_bias_kernel(row_ref, mask_ref, w_ref, o_ref, w_bf_ref):
    width = row_ref.shape[-1]
    x = jnp.broadcast_to(row_ref[0] * LOG2E, (Q_BLOCK, width))
    x = pltpu.roll(x, 0, 1, stride=1, stride_axis=0)
    bias = x[:, Q_BLOCK:] + mask_ref[...]
    col = lax.broadcasted_iota(jnp.int32, bias.shape, 1)
    for t in range(KV_BLOCKS):
        o_ref[t, 0] = jnp.where(col < (KV_BLOCKS - 1 - t) * Q_BLOCK, MASK_VALUE, bias)
    w_bf_ref[...] = w_ref[...].astype(w_bf_ref.dtype)


def _band_mask():
    r = np.arange(Q_BLOCK)[:, None] // CHUNK
    c = np.arange(KV_BLOCKS * Q_BLOCK)[None, :] // CHUNK
    visible = (c >= r) & (c <= r + LEFT_CHUNKS)
    return np.where(visible, 0.0, MASK_VALUE).astype(np.float32)


def _rel_bias(rel_table, cast):
    h, n_rel = rel_table.shape
    kw = KV_BLOCKS * Q_BLOCK
    width = Q_BLOCK + kw
    left = kw - MAX_REL_DIST
    right = width - left - n_rel
    row = jnp.pad(rel_table[:, ::-1], ((0, 0), (left, right)), mode="edge")
    row = row.reshape(h, 1, width)
    cast_in, cast_out, cast_shapes, cast_args = _cast_specs([cast], h)
    return pl.pallas_call(
        _rel_bias_kernel,
        grid=(h,),
        in_specs=[pl.BlockSpec((1, 1, width), lambda i: (i, 0, 0)),
                  pl.BlockSpec((Q_BLOCK, kw), lambda i: (0, 0))] + cast_in,
        out_specs=[pl.BlockSpec((KV_BLOCKS, 1, Q_BLOCK, kw), lambda i: (0, i, 0, 0))] + cast_out,
        out_shape=[jax.ShapeDtypeStruct((KV_BLOCKS, h, Q_BLOCK, kw), F32)] + cast_shapes,
        compiler_params=pltpu.CompilerParams(dimension_semantics=("arbitrary",)),
        name="rel_bias",
    )(row, jnp.asarray(_band_mask()), *cast_args)


def _attn_kernel(q_ref, kp_ref, kc_ref, vp_ref, vc_ref, bias_ref, og_ref, o_ref,
                 acc_ref, vext_ref):
    i = pl.program_id(1)
    n_heads = acc_ref.shape[1] // HEAD_DIM
    nt = (((1,), (1,)), ((), ()))
    win = KV_BLOCKS * Q_BLOCK
    tile = Q_PER_STEP * Q_BLOCK

    @pl.when((pl.program_id(0) == 0) & (i == 0))
    def _():
        vext_ref[:, :, HEAD_DIM:] = jnp.ones(vext_ref.shape[:2] + (HEAD_DIM,), BF16)

    for h in range(n_heads):
        sl = slice(h * HEAD_DIM, (h + 1) * HEAD_DIM)
        vext_ref[h, :tile, :HEAD_DIM] = vp_ref[0, :, sl]
        vext_ref[h, tile:, :HEAD_DIM] = vc_ref[0, :, sl]

    def key_block(n, sl):
        ref = kp_ref if n < Q_PER_STEP else kc_ref
        r0 = (n % Q_PER_STEP) * Q_BLOCK
        return ref[0, r0:r0 + Q_BLOCK, sl]

    for qb in range(Q_PER_STEP):
        rows = slice(qb * Q_BLOCK, (qb + 1) * Q_BLOCK)
        variant = jnp.minimum(i * Q_PER_STEP + qb, KV_BLOCKS - 1)
        for h in range(n_heads):
            sl = slice(h * HEAD_DIM, (h + 1) * HEAD_DIM)
            q = q_ref[0, rows, sl]
            scores = []
            for j in range(KV_BLOCKS):
                s = lax.dot_general(q, key_block(qb + j, sl), nt, preferred_element_type=F32)
                scores.append(s + bias_ref[variant, h, :, j * Q_BLOCK:(j + 1) * Q_BLOCK])
            m = functools.reduce(jnp.maximum, [jnp.max(s, axis=-1, keepdims=True) for s in scores])
            p = jnp.concatenate([jnp.exp2(s - m).astype(BF16) for s in scores], axis=1)
            o = jnp.dot(p, vext_ref[h, qb * Q_BLOCK:qb * Q_BLOCK + win, :],
                        preferred_element_type=F32)
            acc_ref[rows, sl] = o[:, :HEAD_DIM] / o[:, HEAD_DIM:]
        y = acc_ref[rows, :]
        o_ref[0, rows, :] = (y * _rms_scale(y) * og_ref[...]).astype(o_ref.dtype)


def _attention(q3, k3, v3, bias, out_g):
    b, s, aw = q3.shape
    h = bias.shape[1]
    tile = Q_PER_STEP * Q_BLOCK
    blk = (1, tile, aw)
    cur = pl.BlockSpec(blk, lambda bi, i: (bi, i, 0))
    prev = pl.BlockSpec(blk, lambda bi, i: (bi, jnp.maximum(i - 1, 0), 0))
    return pl.pallas_call(
        _attn_kernel,
        grid=(b, s // tile),
        in_specs=[cur, prev, cur, prev, cur, _resident(bias.shape), _resident((1, aw))],
        out_specs=pl.BlockSpec(blk, lambda bi, i: (bi, i, 0)),
        out_shape=jax.ShapeDtypeStruct((b, s, aw), BF16),
        scratch_shapes=[pltpu.VMEM((tile, aw), F32),
                        pltpu.VMEM((h, 2 * tile, 2 * HEAD_DIM), BF16)],
        compiler_params=pltpu.CompilerParams(
            dimension_semantics=("arbitrary", "arbitrary"),
            vmem_limit_bytes=48 * MIB),
        name="attention",
    )(q3, k3, k3, v3, v3, bias, out_g.reshape(1, aw))


def _out_proj_kernel(x_ref, yc_ref, ya_ref, wc_ref, wa_ref, o_ref):
    for r0 in range(0, x_ref.shape[0], ROW_TILE):
        rows = slice(r0, r0 + ROW_TILE)
        acc = jnp.dot(yc_ref[rows, :], wc_ref[...], preferred_element_type=F32)
        acc = acc + jnp.dot(ya_ref[rows, :], wa_ref[...], preferred_element_type=F32)
        o_ref[rows, :] = x_ref[rows, :] + acc


def _out_proj(x2, yc2, ya2, w_out):
    m, d = x2.shape
    cw = yc2.shape[1]
    aw = ya2.shape[1]
    tm = WIDE_TILE
    row_blk = lambda width: pl.BlockSpec((tm, width), lambda i: (i, 0))
    return pl.pallas_call(
        _out_proj_kernel,
        grid=(m // tm,),
        in_specs=[row_blk(d), row_blk(cw), row_blk(aw),
                  _resident((cw, d), (0, 0)), _resident((aw, d), (cw // aw, 0))],
        out_specs=row_blk(d),
        out_shape=jax.ShapeDtypeStruct((m, d), F32),
        compiler_params=pltpu.CompilerParams(
            dimension_semantics=("arbitrary",),
            vmem_limit_bytes=56 * MIB),
        name="out_proj",
    )(x2, yc2, ya2, w_out, w_out)


def _ffn_kernel(x_ref, g_ref, w1_ref, w2_ref, o_ref, h_ref):
    @pl.when(pl.program_id(1) == 0)
    def _():
        for r0 in range(0, x_ref.shape[0], SUB_ROWS):
            x = x_ref[r0:r0 + SUB_ROWS, :]
            h_ref[r0:r0 + SUB_ROWS, :] = (x * _rms_scale(x) * g_ref[...]).astype(BF16)
            o_ref[r0:r0 + SUB_ROWS, :] = x

    for c0 in range(0, w1_ref.shape[1], FFN_CHUNK):
        a = jnp.dot(h_ref[...], w1_ref[:, c0:c0 + FFN_CHUNK], preferred_element_type=F32)
        a = jnp.maximum(a, 0.0)
        a = (a * a).astype(BF16)
        o_ref[...] += jnp.dot(a, w2_ref[c0:c0 + FFN_CHUNK, :], preferred_element_type=F32)


def _ffn(x2, g, w1, w2):
    m, d = x2.shape
    ff = w1.shape[1]
    tm, tf = FFN_ROWS, FFN_COLS
    return pl.pallas_call(
        _ffn_kernel,
        grid=(m // tm, ff // tf),
        in_specs=[
            pl.BlockSpec((tm, d), lambda i, f: (i, 0)),
            _resident((1, d)),
            pl.BlockSpec((d, tf), lambda i, f: (0, f)),
            pl.BlockSpec((tf, d), lambda i, f: (f, 0)),
        ],
        out_specs=pl.BlockSpec((tm, d), lambda i, f: (i, 0)),
        out_shape=jax.ShapeDtypeStruct((m, d), F32),
        scratch_shapes=[pltpu.VMEM((tm, d), BF16)],
        compiler_params=pltpu.CompilerParams(
            dimension_semantics=("arbitrary", "arbitrary"),
            vmem_limit_bytes=56 * MIB),
        name="ffn",
    )(x2, g, w1, w2)


def kernel(x, ln1_g, w_in, w_dw, b_dw, conv_ln_g, conv_ln_b, q_norm_g, k_norm_g, rel_bias,
           out_norm_conv_g, out_norm_attn_g, w_out, ln2_g, w_ff1, w_ff2):
    b, s, d = x.shape
    depth = w_in.shape[0]
    cw = w_dw.shape[2]
    aw = out_norm_attn_g.shape[1]
    ff = w_ff1.shape[2]
    qcol = 2 * cw // aw
    assert cw == aw and w_in.shape[2] == 2 * cw + 3 * aw and aw % HEAD_DIM == 0
    assert s % (Q_PER_STEP * Q_BLOCK) == 0 and s % ROW_TILE == 0 and (b * s) % FFN_ROWS == 0
    assert (b * s) % WIDE_TILE == 0 and ff % FFN_COLS == 0 and rel_bias.shape[2] == 2 * MAX_REL_DIST + 1
    x2 = x.reshape(b * s, d)
    for l in range(depth):
        bias, w_vg = _rel_bias(rel_bias[l], (w_in[l], 2 * cw, 0))
        vg, h, (wq, wk, wv, wo) = _vg_proj(
            x2, ln1_g[l].reshape(1, d), w_vg,
            [(w_in[l], aw, qcol), (w_in[l], aw, qcol + 1), (w_in[l], aw, qcol + 2), (w_out[l], d, 0)])
        q, k, v, (w1, w2) = _qkv_proj(h, wq, wk, wv, q_norm_g[l], k_norm_g[l],
                                      [(w_ff1[l], ff, 0), (w_ff2[l], d, 0)])
        yc = _conv_branch(vg, w_dw[l], b_dw[l], conv_ln_g[l], conv_ln_b[l], out_norm_conv_g[l], s)
        ya = _attention(q.reshape(b, s, aw), k.reshape(b, s, aw), v.reshape(b, s, aw),
                        bias, out_norm_attn_g[l])
        x2 = _out_proj(x2, yc, ya.reshape(b * s, aw), wo)
        x2 = _ffn(x2, ln2_g[l].reshape(1, d), w1, w2)
    return x2.reshape(b, s, d)
```

```python
import functools
import math

import jax
import jax.numpy as jnp
import numpy as np
from jax import lax
from jax.experimental import pallas as pl
from jax.experimental.pallas import tpu as pltpu

F32 = jnp.float32
BF16 = jnp.bfloat16

EPS = 1e-6
CHUNK = 64
LEFT_CHUNKS = 8
HEAD_DIM = 128
CONV_KERNEL = 31
MAX_REL_DIST = 128
LANES = 128
BF16_ROWS = 16
MASK_VALUE = -1e30
LOG2E = math.log2(math.e)

Q_BLOCK = 256
KV_BLOCKS = 3
Q_PER_STEP = 2
HALO = 32
ROW_TILE = 512
WIDE_TILE = 1024
SUB_ROWS = 256
CONV_ROWS = 64
LN_ROWS = 64
NORM_ROWS = 64
MXU_COLS = 512
FFN_ROWS = 512
FFN_COLS = 2048
FFN_CHUNK = 512
MIB = 1024 * 1024


def _rms_scale(x):
    return lax.rsqrt(jnp.mean(x * x, axis=-1, keepdims=True) + EPS)


def _resident(shape, index=None):
    index = index or (0,) * len(shape)
    return pl.BlockSpec(shape, lambda *_: index, pipeline_mode=pl.Buffered(1))


def _cast_specs(casts, steps):
    in_specs, out_specs, shapes, args = [], [], [], []
    for arr, width, col in casts:
        assert arr.shape[0] % steps == 0 and (col + 1) * width <= arr.shape[1]
        slab = arr.shape[0] // steps
        in_specs.append(pl.BlockSpec((slab, width), lambda i, col=col: (i, col)))
        out_specs.append(pl.BlockSpec((slab, width), lambda i: (i, 0)))
        shapes.append(jax.ShapeDtypeStruct((arr.shape[0], width), BF16))
        args.append(arr)
    return in_specs, out_specs, shapes, args


def _vg_proj_kernel(x_ref, g_ref, w_ref, *refs, n_cast):
    cast_in = refs[:n_cast]
    vg_ref, h_ref = refs[n_cast:n_cast + 2]
    cast_out = refs[n_cast + 2:]
    tm = x_ref.shape[0]
    n = w_ref.shape[1]

    def norm_rows(r0, nrows):
        x = x_ref[r0:r0 + nrows, :]
        h_ref[r0:r0 + nrows, :] = (x * _rms_scale(x) * g_ref[...]).astype(BF16)

    for r0 in range(0, tm, NORM_ROWS):
        norm_rows(r0, NORM_ROWS)
    for s in range(tm // SUB_ROWS):
        rows = slice(s * SUB_ROWS, (s + 1) * SUB_ROWS)
        for c0 in range(0, n, MXU_COLS):
            y = jnp.dot(h_ref[rows, :], w_ref[:, c0:c0 + MXU_COLS], preferred_element_type=F32)
            for g in range(MXU_COLS // LANES):
                vg_ref[c0 // LANES + g, rows, :] = y[:, g * LANES:(g + 1) * LANES].astype(vg_ref.dtype)
    for src, dst in zip(cast_in, cast_out):
        dst[...] = src[...].astype(dst.dtype)


def _vg_proj(x2, g, w_vg, casts):
    m, d = x2.shape
    n = w_vg.shape[1]
    steps = m // WIDE_TILE
    cast_in, cast_out, cast_shapes, cast_args = _cast_specs(casts, steps)
    outs = pl.pallas_call(
        functools.partial(_vg_proj_kernel, n_cast=len(casts)),
        grid=(steps,),
        in_specs=[pl.BlockSpec((WIDE_TILE, d), lambda i: (i, 0)),
                  _resident((1, d)), _resident(w_vg.shape)] + cast_in,
        out_specs=[pl.BlockSpec((n // LANES, WIDE_TILE, LANES), lambda i: (0, i, 0)),
                   pl.BlockSpec((WIDE_TILE, d), lambda i: (i, 0))] + cast_out,
        out_shape=[jax.ShapeDtypeStruct((n // LANES, m, LANES), BF16),
                   jax.ShapeDtypeStruct((m, d), BF16)] + cast_shapes,
        compiler_params=pltpu.CompilerParams(
            dimension_semantics=("arbitrary",),
            vmem_limit_bytes=56 * MIB),
        name="vg_proj",
    )(x2, g, w_vg, *cast_args)
    return outs[0], outs[1], outs[2:]


def _qkv_proj_kernel(h_ref, wq_ref, wk_ref, wv_ref, qg_ref, kg_ref, *refs, n_cast):
    cast_in = refs[:n_cast]
    q_ref, k_ref, v_ref = refs[n_cast:n_cast + 3]
    cast_out = refs[n_cast + 3:]
    tm = h_ref.shape[0]
    aw = q_ref.shape[1]

    def head_norm(y, gain, out_ref, rows):
        for hd in range(aw // HEAD_DIM):
            sl = slice(hd * HEAD_DIM, (hd + 1) * HEAD_DIM)
            blk = y[:, sl]
            out_ref[rows, sl] = (blk * _rms_scale(blk) * gain).astype(out_ref.dtype)

    qg = qg_ref[...] * (HEAD_DIM ** -0.5 * LOG2E)
    kg = kg_ref[...]
    for s in range(tm // SUB_ROWS):
        rows = slice(s * SUB_ROWS, (s + 1) * SUB_ROWS)
        h = h_ref[rows, :]
        head_norm(jnp.dot(h, wq_ref[...], preferred_element_type=F32), qg, q_ref, rows)
        head_norm(jnp.dot(h, wk_ref[...], preferred_element_type=F32), kg, k_ref, rows)
        v_ref[rows, :] = jnp.dot(h, wv_ref[...], preferred_element_type=F32).astype(v_ref.dtype)
    for src, dst in zip(cast_in, cast_out):
        dst[...] = src[...].astype(dst.dtype)


def _qkv_proj(h2, wq, wk, wv, q_g, k_g, casts):
    m, d = h2.shape
    aw = wq.shape[1]
    steps = m // ROW_TILE
    row_blk = lambda width: pl.BlockSpec((ROW_TILE, width), lambda i: (i, 0))
    cast_in, cast_out, cast_shapes, cast_args = _cast_specs(casts, steps)
    outs = pl.pallas_call(
        functools.partial(_qkv_proj_kernel, n_cast=len(casts)),
        grid=(steps,),
        in_specs=[row_blk(d), _resident(wq.shape), _resident(wk.shape), _resident(wv.shape),
                  _resident((1, HEAD_DIM)), _resident((1, HEAD_DIM))] + cast_in,
        out_specs=[row_blk(aw)] * 3 + cast_out,
        out_shape=[jax.ShapeDtypeStruct((m, aw), BF16)] * 3 + cast_shapes,
        compiler_params=pltpu.CompilerParams(
            dimension_semantics=("arbitrary",),
            vmem_limit_bytes=48 * MIB),
        name="qkv_proj",
    )(h2, wq, wk, wv, q_g.reshape(1, HEAD_DIM), k_g.reshape(1, HEAD_DIM), *cast_args)
    return outs[0], outs[1], outs[2], outs[3:]


def _conv_kernel(vg_ref, halo_ref, cw_ref, cb_ref, lng_ref, lnb_ref, og_ref, o_ref,
                 hb_ref, hs_ref, y_ref, *, tiles_per_seq):
    groups = y_ref.shape[0]
    ts = y_ref.shape[1]
    width = groups * LANES
    first_tile = pl.program_id(0) % tiles_per_seq == 0
    first = HALO - (CONV_KERNEL - 1)

    def group_body(c, carry):
        halo = halo_ref[c].astype(F32) * jax.nn.sigmoid(halo_ref[groups + c].astype(F32))
        hb_ref[0:HALO, :] = jnp.where(first_tile, 0.0, halo)
        hb_ref[HALO:HALO + ts, :] = vg_ref[c].astype(F32) * jax.nn.sigmoid(vg_ref[groups + c].astype(F32))
        hb_ref[HALO + ts:, :] = jnp.zeros((BF16_ROWS, LANES), F32)
        for p in range(BF16_ROWS):
            hs_ref[p, :, :] = hb_ref[p:p + HALO + ts, :].astype(BF16)
        for r in range(0, ts, CONV_ROWS):
            prods = []
            for t in range(CONV_KERNEL):
                a, p = divmod(first + t, BF16_ROWS)
                lo = r + a * BF16_ROWS
                prods.append(cw_ref[c, t].astype(F32) * hs_ref[p, lo:lo + CONV_ROWS, :].astype(F32))
            y_ref[c, r:r + CONV_ROWS, :] = functools.reduce(lambda u, v: u + v, prods) + cb_ref[c]
        return carry

    lax.fori_loop(0, groups, group_body, 0)

    for r in range(0, ts, LN_ROWS):
        rows = slice(r, r + LN_ROWS)
        y = y_ref[:, rows, :]
        mu = jnp.sum(jnp.sum(y, axis=0), axis=-1, keepdims=True) / width
        yc = y - mu[None]
        var = jnp.sum(jnp.sum(yc * yc, axis=0), axis=-1, keepdims=True) / width
        z = yc * lax.rsqrt(var + EPS)[None] * lng_ref[...] + lnb_ref[...]
        sw = z * jax.nn.sigmoid(z)
        ms = jnp.sum(jnp.sum(sw * sw, axis=0), axis=-1, keepdims=True) / width
        out = sw * lax.rsqrt(ms + EPS)[None] * og_ref[...]
        for c in range(groups):
            o_ref[rows, c * LANES:(c + 1) * LANES] = out[c].astype(o_ref.dtype)


def _conv_branch(vg3, w_dw, b_dw, ln_g, ln_b, out_g, seq_len):
    n_groups2, m, _ = vg3.shape
    cw = w_dw.shape[1]
    groups = cw // LANES
    ts = ROW_TILE
    w_g = w_dw.reshape(CONV_KERNEL, groups, LANES).transpose(1, 0, 2)
    w_g = jnp.broadcast_to(w_g[:, :, None, :], (groups, CONV_KERNEL, CONV_ROWS, LANES)).astype(BF16)
    as_groups = lambda v: v.reshape(groups, 1, LANES)
    per_halo = ts // HALO
    return pl.pallas_call(
        functools.partial(_conv_kernel, tiles_per_seq=seq_len // ts),
        grid=(m // ts,),
        in_specs=[
            pl.BlockSpec((n_groups2, ts, LANES), lambda i: (0, i, 0)),
            pl.BlockSpec((n_groups2, HALO, LANES), lambda i: (0, jnp.maximum(i * per_halo - 1, 0), 0)),
            _resident((groups, CONV_KERNEL, CONV_ROWS, LANES)),
            _resident((groups, 1, LANES)), _resident((groups, 1, LANES)),
            _resident((groups, 1, LANES)), _resident((groups, 1, LANES)),
        ],
        out_specs=pl.BlockSpec((ts, cw), lambda i: (i, 0)),
        out_shape=jax.ShapeDtypeStruct((m, cw), BF16),
        scratch_shapes=[pltpu.VMEM((HALO + ts + BF16_ROWS, LANES), F32),
                        pltpu.VMEM((BF16_ROWS, HALO + ts, LANES), BF16),
                        pltpu.VMEM((groups, ts, LANES), F32)],
        compiler_params=pltpu.CompilerParams(
            dimension_semantics=("arbitrary",),
            vmem_limit_bytes=32 * MIB),
        name="conv_branch",
    )(vg3, vg3, w_g, as_groups(b_dw), as_groups(ln_g), as_groups(ln_b), as_groups(out_g))


def _rel_bias_kernel(row_ref, mask_ref, w_ref, o_ref, w_bf_ref):
    width = row_ref.shape[-1]
    x = jnp.broadcast_to(row_ref[0] * LOG2E, (Q_BLOCK, width))
    x = pltpu.roll(x, 0, 1, stride=1, stride_axis=0)
    bias = x[:, Q_BLOCK:] + mask_ref[...]
    col = lax.broadcasted_iota(jnp.int32, bias.shape, 1)
    for t in range(KV_BLOCKS):
        o_ref[t, 0] = jnp.where(col < (KV_BLOCKS - 1 - t) * Q_BLOCK, MASK_VALUE, bias)
    w_bf_ref[...] = w_ref[...].astype(w_bf_ref.dtype)


def _band_mask():
    r = np.arange(Q_BLOCK)[:, None] // CHUNK
    c = np.arange(KV_BLOCKS * Q_BLOCK)[None, :] // CHUNK
    visible = (c >= r) & (c <= r + LEFT_CHUNKS)
    return np.where(visible, 0.0, MASK_VALUE).astype(np.float32)


def _rel_bias(rel_table, cast):
    h, n_rel = rel_table.shape
    kw = KV_BLOCKS * Q_BLOCK
    width = Q_BLOCK + kw
    left = kw - MAX_REL_DIST
    right = width - left - n_rel
    row = jnp.pad(rel_table[:, ::-1], ((0, 0), (left, right)), mode="edge")
    row = row.reshape(h, 1, width)
    cast_in, cast_out, cast_shapes, cast_args = _cast_specs([cast], h)
    return pl.pallas_call(
        _rel_bias_kernel,
        grid=(h,),
        in_specs=[pl.BlockSpec((1, 1, width), lambda i: (i, 0, 0)),
                  pl.BlockSpec((Q_BLOCK, kw), lambda i: (0, 0))] + cast_in,
        out_specs=[pl.BlockSpec((KV_BLOCKS, 1, Q_BLOCK, kw), lambda i: (0, i, 0, 0))] + cast_out,
        out_shape=[jax.ShapeDtypeStruct((KV_BLOCKS, h, Q_BLOCK, kw), F32)] + cast_shapes,
        compiler_params=pltpu.CompilerParams(dimension_semantics=("arbitrary",)),
        name="rel_bias",
    )(row, jnp.asarray(_band_mask()), *cast_args)


def _attn_kernel(q_ref, kp_ref, kc_ref, vp_ref, vc_ref, bias_ref, og_ref, o_ref,
                 acc_ref, vext_ref):
    i = pl.program_id(1)
    n_heads = acc_ref.shape[1] // HEAD_DIM
    nt = (((1,), (1,)), ((), ()))
    win = KV_BLOCKS * Q_BLOCK
    tile = Q_PER_STEP * Q_BLOCK

    @pl.when((pl.program_id(0) == 0) & (i == 0))
    def _():
        vext_ref[:, :, HEAD_DIM:] = jnp.ones(vext_ref.shape[:2] + (HEAD_DIM,), BF16)

    for h in range(n_heads):
        sl = slice(h * HEAD_DIM, (h + 1) * HEAD_DIM)
        vext_ref[h, :tile, :HEAD_DIM] = vp_ref[0, :, sl]
        vext_ref[h, tile:, :HEAD_DIM] = vc_ref[0, :, sl]

    def key_block(n, sl):
        ref = kp_ref if n < Q_PER_STEP else kc_ref
        r0 = (n % Q_PER_STEP) * Q_BLOCK
        return ref[0, r0:r0 + Q_BLOCK, sl]

    for qb in range(Q_PER_STEP):
        rows = slice(qb * Q_BLOCK, (qb + 1) * Q_BLOCK)
        variant = jnp.minimum(i * Q_PER_STEP + qb, KV_BLOCKS - 1)
        for h in range(n_heads):
            sl = slice(h * HEAD_DIM, (h + 1) * HEAD_DIM)
            q = q_ref[0, rows, sl]
            scores = []
            for j in range(KV_BLOCKS):
                s = lax.dot_general(q, key_block(qb + j, sl), nt, preferred_element_type=F32)
                scores.append(s + bias_ref[variant, h, :, j * Q_BLOCK:(j + 1) * Q_BLOCK])
            m = functools.reduce(jnp.maximum, [jnp.max(s, axis=-1, keepdims=True) for s in scores])
            p = jnp.concatenate([jnp.exp2(s - m).astype(BF16) for s in scores], axis=1)
            o = jnp.dot(p, vext_ref[h, qb * Q_BLOCK:qb * Q_BLOCK + win, :],
                        preferred_element_type=F32)
            acc_ref[rows, sl] = o[:, :HEAD_DIM] / o[:, HEAD_DIM:]
        y = acc_ref[rows, :]
        o_ref[0, rows, :] = (y * _rms_scale(y) * og_ref[...]).astype(o_ref.dtype)


def _attention(q3, k3, v3, bias, out_g):
    b, s, aw = q3.shape
    h = bias.shape[1]
    tile = Q_PER_STEP * Q_BLOCK
    blk = (1, tile, aw)
    cur = pl.BlockSpec(blk, lambda bi, i: (bi, i, 0))
    prev = pl.BlockSpec(blk, lambda bi, i: (bi, jnp.maximum(i - 1, 0), 0))
    return pl.pallas_call(
        _attn_kernel,
        grid=(b, s // tile),
        in_specs=[cur, prev, cur, prev, cur, _resident(bias.shape), _resident((1, aw))],
        out_specs=pl.BlockSpec(blk, lambda bi, i: (bi, i, 0)),
        out_shape=jax.ShapeDtypeStruct((b, s, aw), BF16),
        scratch_shapes=[pltpu.VMEM((tile, aw), F32),
                        pltpu.VMEM((h, 2 * tile, 2 * HEAD_DIM), BF16)],
        compiler_params=pltpu.CompilerParams(
            dimension_semantics=("arbitrary", "arbitrary"),
            vmem_limit_bytes=48 * MIB),
        name="attention",
    )(q3, k3, k3, v3, v3, bias, out_g.reshape(1, aw))


def _out_proj_kernel(x_ref, yc_ref, ya_ref, wc_ref, wa_ref, o_ref):
    for r0 in range(0, x_ref.shape[0], ROW_TILE):
        rows = slice(r0, r0 + ROW_TILE)
        acc = jnp.dot(yc_ref[rows, :], wc_ref[...], preferred_element_type=F32)
        acc = acc + jnp.dot(ya_ref[rows, :], wa_ref[...], preferred_element_type=F32)
        o_ref[rows, :] = x_ref[rows, :] + acc


def _out_proj(x2, yc2, ya2, w_out):
    m, d = x2.shape
    cw = yc2.shape[1]
    aw = ya2.shape[1]
    tm = WIDE_TILE
    row_blk = lambda width: pl.BlockSpec((tm, width), lambda i: (i, 0))
    return pl.pallas_call(
        _out_proj_kernel,
        grid=(m // tm,),
        in_specs=[row_blk(d), row_blk(cw), row_blk(aw),
                  _resident((cw, d), (0, 0)), _resident((aw, d), (cw // aw, 0))],
        out_specs=row_blk(d),
        out_shape=jax.ShapeDtypeStruct((m, d), F32),
        compiler_params=pltpu.CompilerParams(
            dimension_semantics=("arbitrary",),
            vmem_limit_bytes=56 * MIB),
        name="out_proj",
    )(x2, yc2, ya2, w_out, w_out)


def _ffn_kernel(x_ref, g_ref, w1_ref, w2_ref, o_ref, h_ref):
    @pl.when(pl.program_id(1) == 0)
    def _():
        for r0 in range(0, x_ref.shape[0], SUB_ROWS):
            x = x_ref[r0:r0 + SUB_ROWS, :]
            h_ref[r0:r0 + SUB_ROWS, :] = (x * _rms_scale(x) * g_ref[...]).astype(BF16)
            o_ref[r0:r0 + SUB_ROWS, :] = x

    for c0 in range(0, w1_ref.shape[1], FFN_CHUNK):
        a = jnp.dot(h_ref[...], w1_ref[:, c0:c0 + FFN_CHUNK], preferred_element_type=F32)
        a = jnp.maximum(a, 0.0)
        a = (a * a).astype(BF16)
        o_ref[...] += jnp.dot(a, w2_ref[c0:c0 + FFN_CHUNK, :], preferred_element_type=F32)


def _ffn(x2, g, w1, w2):
    m, d = x2.shape
    ff = w1.shape[1]
    tm, tf = FFN_ROWS, FFN_COLS
    return pl.pallas_call(
        _ffn_kernel,
        grid=(m // tm, ff // tf),
        in_specs=[
            pl.BlockSpec((tm, d), lambda i, f: (i, 0)),
            _resident((1, d)),
            pl.BlockSpec((d, tf), lambda i, f: (0, f)),
            pl.BlockSpec((tf, d), lambda i, f: (f, 0)),
        ],
        out_specs=pl.BlockSpec((tm, d), lambda i, f: (i, 0)),
        out_shape=jax.ShapeDtypeStruct((m, d), F32),
        scratch_shapes=[pltpu.VMEM((tm, d), BF16)],
        compiler_params=pltpu.CompilerParams(
            dimension_semantics=("arbitrary", "arbitrary"),
            vmem_limit_bytes=56 * MIB),
        name="ffn",
    )(x2, g, w1, w2)


def kernel(x, ln1_g, w_in, w_dw, b_dw, conv_ln_g, conv_ln_b, q_norm_g, k_norm_g, rel_bias,
           out_norm_conv_g, out_norm_attn_g, w_out, ln2_g, w_ff1, w_ff2):
    b, s, d = x.shape
    depth = w_in.shape[0]
    cw = w_dw.shape[2]
    aw = out_norm_attn_g.shape[1]
    ff = w_ff1.shape[2]
    qcol = 2 * cw // aw
    assert cw == aw and w_in.shape[2] == 2 * cw + 3 * aw and aw % HEAD_DIM == 0
    assert s % (Q_PER_STEP * Q_BLOCK) == 0 and s % ROW_TILE == 0 and (b * s) % FFN_ROWS == 0
    assert (b * s) % WIDE_TILE == 0 and ff % FFN_COLS == 0 and rel_bias.shape[2] == 2 * MAX_REL_DIST + 1
    x2 = x.reshape(b * s, d)
    for l in range(depth):
        bias, w_vg = _rel_bias(rel_bias[l], (w_in[l], 2 * cw, 0))
        vg, h, (wq, wk, wv, wo) = _vg_proj(
            x2, ln1_g[l].reshape(1, d), w_vg,
            [(w_in[l], aw, qcol), (w_in[l], aw, qcol + 1), (w_in[l], aw, qcol + 2), (w_out[l], d, 0)])
        q, k, v, (w1, w2) = _qkv_proj(h, wq, wk, wv, q_norm_g[l], k_norm_g[l],
                                      [(w_ff1[l], ff, 0), (w_ff2[l], d, 0)])
        yc = _conv_branch(vg, w_dw[l], b_dw[l], conv_ln_g[l], conv_ln_b[l], out_norm_conv_g[l], s)
        ya = _attention(q.reshape(b, s, aw), k.reshape(b, s, aw), v.reshape(b, s, aw),
                        bias, out_norm_attn_g[l])
        x2 = _out_proj(x2, yc, ya.reshape(b * s, aw), wo)
        x2 = _ffn(x2, ln2_g[l].reshape(1, d), w1, w2)
    return x2.reshape(b, s, d)
```

```python
import functools
import math

import jax
import jax.numpy as jnp
import numpy as np
from jax import lax
from jax.experimental import pallas as pl
from jax.experimental.pallas import tpu as pltpu

F32 = jnp.float32
BF16 = jnp.bfloat16

EPS = 1e-6
CHUNK = 64
LEFT_CHUNKS = 8
HEAD_DIM = 128
CONV_KERNEL = 31
MAX_REL_DIST = 128
LANES = 128
BF16_ROWS = 16
MASK_VALUE = -1e30
LOG2E = math.log2(math.e)

Q_BLOCK = 256
KV_BLOCKS = 3
Q_PER_STEP = 2
HALO = 32
ROW_TILE = 512
WIDE_TILE = 1024
SUB_ROWS = 256
CONV_ROWS = 64
LN_ROWS = 64
NORM_ROWS = 64
MXU_COLS = 512
FFN_ROWS = 512
FFN_COLS = 2048
FFN_CHUNK = 512
MIB = 1024 * 1024


def _rms_scale(x):
    return lax.rsqrt(jnp.mean(x * x, axis=-1, keepdims=True) + EPS)


def _resident(shape, index=None):
    index = index or (0,) * len(shape)
    return pl.BlockSpec(shape, lambda *_: index, pipeline_mode=pl.Buffered(1))


def _cast_specs(casts, steps):
    in_specs, out_specs, shapes, args = [], [], [], []
    for arr, width, col in casts:
        assert arr.shape[0] % steps == 0 and (col + 1) * width <= arr.shape[1]
        slab = arr.shape[0] // steps
        in_specs.append(pl.BlockSpec((slab, width), lambda i, col=col: (i, col)))
        out_specs.append(pl.BlockSpec((slab, width), lambda i: (i, 0)))
        shapes.append(jax.ShapeDtypeStruct((arr.shape[0], width), BF16))
        args.append(arr)
    return in_specs, out_specs, shapes, args


def _vg_proj_kernel(x_ref, g_ref, w_ref, *refs, n_cast):
    cast_in = refs[:n_cast]
    vg_ref, h_ref = refs[n_cast:n_cast + 2]
    cast_out = refs[n_cast + 2:]
    tm = x_ref.shape[0]
    n = w_ref.shape[1]

    def norm_rows(r0, nrows):
        x = x_ref[r0:r0 + nrows, :]
        h_ref[r0:r0 + nrows, :] = (x * _rms_scale(x) * g_ref[...]).astype(BF16)

    for r0 in range(0, tm, NORM_ROWS):
        norm_rows(r0, NORM_ROWS)
    for s in range(tm // SUB_ROWS):
        rows = slice(s * SUB_ROWS, (s + 1) * SUB_ROWS)
        for c0 in range(0, n, MXU_COLS):
            y = jnp.dot(h_ref[rows, :], w_ref[:, c0:c0 + MXU_COLS], preferred_element_type=F32)
            for g in range(MXU_COLS // LANES):
                vg_ref[c0 // LANES + g, rows, :] = y[:, g * LANES:(g + 1) * LANES].astype(vg_ref.dtype)
    for src, dst in zip(cast_in, cast_out):
        dst[...] = src[...].astype(dst.dtype)


def _vg_proj(x2, g, w_vg, casts):
    m, d = x2.shape
    n = w_vg.shape[1]
    steps = m // WIDE_TILE
    cast_in, cast_out, cast_shapes, cast_args = _cast_specs(casts, steps)
    outs = pl.pallas_call(
        functools.partial(_vg_proj_kernel, n_cast=len(casts)),
        grid=(steps,),
        in_specs=[pl.BlockSpec((WIDE_TILE, d), lambda i: (i, 0)),
                  _resident((1, d)), _resident(w_vg.shape)] + cast_in,
        out_specs=[pl.BlockSpec((n // LANES, WIDE_TILE, LANES), lambda i: (0, i, 0)),
                   pl.BlockSpec((WIDE_TILE, d), lambda i: (i, 0))] + cast_out,
        out_shape=[jax.ShapeDtypeStruct((n // LANES, m, LANES), BF16),
                   jax.ShapeDtypeStruct((m, d), BF16)] + cast_shapes,
        compiler_params=pltpu.CompilerParams(
            dimension_semantics=("arbitrary",),
            vmem_limit_bytes=56 * MIB),
        name="vg_proj",
    )(x2, g, w_vg, *cast_args)
    return outs[0], outs[1], outs[2:]


def _qkv_proj_kernel(h_ref, wq_ref, wk_ref, wv_ref, qg_ref, kg_ref, *refs, n_cast):
    cast_in = refs[:n_cast]
    q_ref, k_ref, v_ref = refs[n_cast:n_cast + 3]
    cast_out = refs[n_cast + 3:]
    tm = h_ref.shape[0]
    aw = q_ref.shape[1]

    def head_norm(y, gain, out_ref, rows):
        for hd in range(aw // HEAD_DIM):
            sl = slice(hd * HEAD_DIM, (hd + 1) * HEAD_DIM)
            blk = y[:, sl]
            out_ref[rows, sl] = (blk * _rms_scale(blk) * gain).astype(out_ref.dtype)

    qg = qg_ref[...] * (HEAD_DIM ** -0.5 * LOG2E)
    kg = kg_ref[...]
    for s in range(tm // SUB_ROWS):
        rows = slice(s * SUB_ROWS, (s + 1) * SUB_ROWS)
        h = h_ref[rows, :]
        head_norm(jnp.dot(h, wq_ref[...], preferred_element_type=F32), qg, q_ref, rows)
        head_norm(jnp.dot(h, wk_ref[...], preferred_element_type=F32), kg, k_ref, rows)
        v_ref[rows, :] = jnp.dot(h, wv_ref[...], preferred_element_type=F32).astype(v_ref.dtype)
    for src, dst in zip(cast_in, cast_out):
        dst[...] = src[...].astype(dst.dtype)


def _qkv_proj(h2, wq, wk, wv, q_g, k_g, casts):
    m, d = h2.shape
    aw = wq.shape[1]
    steps = m // ROW_TILE
    row_blk = lambda width: pl.BlockSpec((ROW_TILE, width), lambda i: (i, 0))
    cast_in, cast_out, cast_shapes, cast_args = _cast_specs(casts, steps)
    outs = pl.pallas_call(
        functools.partial(_qkv_proj_kernel, n_cast=len(casts)),
        grid=(steps,),
        in_specs=[row_blk(d), _resident(wq.shape), _resident(wk.shape), _resident(wv.shape),
                  _resident((1, HEAD_DIM)), _resident((1, HEAD_DIM))] + cast_in,
        out_specs=[row_blk(aw)] * 3 + cast_out,
        out_shape=[jax.ShapeDtypeStruct((m, aw), BF16)] * 3 + cast_shapes,
        compiler_params=pltpu.CompilerParams(
            dimension_semantics=("arbitrary",),
            vmem_limit_bytes=48 * MIB),
        name="qkv_proj",
    )(h2, wq, wk, wv, q_g.reshape(1, HEAD_DIM), k_g.reshape(1, HEAD_DIM), *cast_args)
    return outs[0], outs[1], outs[2], outs[3:]


def _conv_kernel(vg_ref, halo_ref, cw_ref, cb_ref, lng_ref, lnb_ref, og_ref, *refs,
                 n_cast, tiles_per_seq):
    cast_in = refs[:n_cast]
    o_ref = refs[n_cast]
    cast_out = refs[n_cast + 1:2 * n_cast + 1]
    hb_ref, hs_ref, y_ref = refs[2 * n_cast + 1:]
    for src, dst in zip(cast_in, cast_out):
        dst[...] = src[...].astype(dst.dtype)
    groups = y_ref.shape[0]
    ts = y_ref.shape[1]
    width = groups * LANES
    first_tile = pl.program_id(0) % tiles_per_seq == 0
    first = HALO - (CONV_KERNEL - 1)

    def group_body(c, carry):
        halo = halo_ref[c].astype(F32) * jax.nn.sigmoid(halo_ref[groups + c].astype(F32))
        hb_ref[0:HALO, :] = jnp.where(first_tile, 0.0, halo)
        hb_ref[HALO:HALO + ts, :] = vg_ref[c].astype(F32) * jax.nn.sigmoid(vg_ref[groups + c].astype(F32))
        hb_ref[HALO + ts:, :] = jnp.zeros((BF16_ROWS, LANES), F32)
        for p in range(BF16_ROWS):
            hs_ref[p, :, :] = hb_ref[p:p + HALO + ts, :].astype(BF16)
        for r in range(0, ts, CONV_ROWS):
            prods = []
            for t in range(CONV_KERNEL):
                a, p = divmod(first + t, BF16_ROWS)
                lo = r + a * BF16_ROWS
                prods.append(cw_ref[c, t].astype(F32) * hs_ref[p, lo:lo + CONV_ROWS, :].astype(F32))
            y_ref[c, r:r + CONV_ROWS, :] = functools.reduce(lambda u, v: u + v, prods) + cb_ref[c]
        return carry

    lax.fori_loop(0, groups, group_body, 0)

    for r in range(0, ts, LN_ROWS):
        rows = slice(r, r + LN_ROWS)
        y = y_ref[:, rows, :]
        mu = jnp.sum(jnp.sum(y, axis=0), axis=-1, keepdims=True) / width
        yc = y - mu[None]
        var = jnp.sum(jnp.sum(yc * yc, axis=0), axis=-1, keepdims=True) / width
        z = yc * lax.rsqrt(var + EPS)[None] * lng_ref[...] + lnb_ref[...]
        sw = z * jax.nn.sigmoid(z)
        ms = jnp.sum(jnp.sum(sw * sw, axis=0), axis=-1, keepdims=True) / width
        out = sw * lax.rsqrt(ms + EPS)[None] * og_ref[...]
        for c in range(groups):
            o_ref[rows, c * LANES:(c + 1) * LANES] = out[c].astype(o_ref.dtype)


def _conv_branch(vg3, w_dw, b_dw, ln_g, ln_b, out_g, seq_len, casts):
    n_groups2, m, _ = vg3.shape
    cw = w_dw.shape[1]
    groups = cw // LANES
    ts = ROW_TILE
    w_g = w_dw.reshape(CONV_KERNEL, groups, LANES).transpose(1, 0, 2)
    w_g = jnp.broadcast_to(w_g[:, :, None, :], (groups, CONV_KERNEL, CONV_ROWS, LANES)).astype(BF16)
    as_groups = lambda v: v.reshape(groups, 1, LANES)
    per_halo = ts // HALO
    cast_in, cast_out, cast_shapes, cast_args = _cast_specs(casts, m // ts)
    outs = pl.pallas_call(
        functools.partial(_conv_kernel, n_cast=len(casts), tiles_per_seq=seq_len // ts),
        grid=(m // ts,),
        in_specs=[
            pl.BlockSpec((n_groups2, ts, LANES), lambda i: (0, i, 0)),
            pl.BlockSpec((n_groups2, HALO, LANES), lambda i: (0, jnp.maximum(i * per_halo - 1, 0), 0)),
            _resident((groups, CONV_KERNEL, CONV_ROWS, LANES)),
            _resident((groups, 1, LANES)), _resident((groups, 1, LANES)),
            _resident((groups, 1, LANES)), _resident((groups, 1, LANES)),
        ] + cast_in,
        out_specs=[pl.BlockSpec((ts, cw), lambda i: (i, 0))] + cast_out,
        out_shape=[jax.ShapeDtypeStruct((m, cw), BF16)] + cast_shapes,
        scratch_shapes=[pltpu.VMEM((HALO + ts + BF16_ROWS, LANES), F32),
                        pltpu.VMEM((BF16_ROWS, HALO + ts, LANES), BF16),
                        pltpu.VMEM((groups, ts, LANES), F32)],
        compiler_params=pltpu.CompilerParams(
            dimension_semantics=("arbitrary",),
            vmem_limit_bytes=32 * MIB),
        name="conv_branch",
    )(vg3, vg3, w_g, as_groups(b_dw), as_groups(ln_g), as_groups(ln_b), as_groups(out_g), *cast_args)
    return outs[0], outs[1:]


def _rel_bias_kernel(row_ref, mask_ref, w_ref, o_ref, w_bf_ref):
    width = row_ref.shape[-1]
    x = jnp.broadcast_to(row_ref[0] * LOG2E, (Q_BLOCK, width))
    x = pltpu.roll(x, 0, 1, stride=1, stride_axis=0)
    bias = x[:, Q_BLOCK:] + mask_ref[...]
    col = lax.broadcasted_iota(jnp.int32, bias.shape, 1)
    for t in range(KV_BLOCKS):
        o_ref[t, 0] = jnp.where(col < (KV_BLOCKS - 1 - t) * Q_BLOCK, MASK_VALUE, bias)
    w_bf_ref[...] = w_ref[...].astype(w_bf_ref.dtype)


def _band_mask():
    r = np.arange(Q_BLOCK)[:, None] // CHUNK
    c = np.arange(KV_BLOCKS * Q_BLOCK)[None, :] // CHUNK
    visible = (c >= r) & (c <= r + LEFT_CHUNKS)
    return np.where(visible, 0.0, MASK_VALUE).astype(np.float32)


def _rel_bias(rel_table, cast):
    h, n_rel = rel_table.shape
    kw = KV_BLOCKS * Q_BLOCK
    width = Q_BLOCK + kw
    left = kw - MAX_REL_DIST
    right = width - left - n_rel
    row = jnp.pad(rel_table[:, ::-1], ((0, 0), (left, right)), mode="edge")
    row = row.reshape(h, 1, width)
    cast_in, cast_out, cast_shapes, cast_args = _cast_specs([cast], h)
    return pl.pallas_call(
        _rel_bias_kernel,
        grid=(h,),
        in_specs=[pl.BlockSpec((1, 1, width), lambda i: (i, 0, 0)),
                  pl.BlockSpec((Q_BLOCK, kw), lambda i: (0, 0))] + cast_in,
        out_specs=[pl.BlockSpec((KV_BLOCKS, 1, Q_BLOCK, kw), lambda i: (0, i, 0, 0))] + cast_out,
        out_shape=[jax.ShapeDtypeStruct((KV_BLOCKS, h, Q_BLOCK, kw), F32)] + cast_shapes,
        compiler_params=pltpu.CompilerParams(dimension_semantics=("arbitrary",)),
        name="rel_bias",
    )(row, jnp.asarray(_band_mask()), *cast_args)


def _attn_kernel(q_ref, kp_ref, kc_ref, vp_ref, vc_ref, bias_ref, og_ref, o_ref,
                 acc_ref, vext_ref):
    i = pl.program_id(1)
    n_heads = acc_ref.shape[1] // HEAD_DIM
    nt = (((1,), (1,)), ((), ()))
    win = KV_BLOCKS * Q_BLOCK
    tile = Q_PER_STEP * Q_BLOCK

    @pl.when((pl.program_id(0) == 0) & (i == 0))
    def _():
        vext_ref[:, :, HEAD_DIM:] = jnp.ones(vext_ref.shape[:2] + (HEAD_DIM,), BF16)

    for h in range(n_heads):
        sl = slice(h * HEAD_DIM, (h + 1) * HEAD_DIM)
        vext_ref[h, :tile, :HEAD_DIM] = vp_ref[0, :, sl]
        vext_ref[h, tile:, :HEAD_DIM] = vc_ref[0, :, sl]

    def key_block(n, sl):
        ref = kp_ref if n < Q_PER_STEP else kc_ref
        r0 = (n % Q_PER_STEP) * Q_BLOCK
        return ref[0, r0:r0 + Q_BLOCK, sl]

    for qb in range(Q_PER_STEP):
        rows = slice(qb * Q_BLOCK, (qb + 1) * Q_BLOCK)
        variant = jnp.minimum(i * Q_PER_STEP + qb, KV_BLOCKS - 1)
        for h in range(n_heads):
            sl = slice(h * HEAD_DIM, (h + 1) * HEAD_DIM)
            q = q_ref[0, rows, sl]
            scores = []
            for j in range(KV_BLOCKS):
                s = lax.dot_general(q, key_block(qb + j, sl), nt, preferred_element_type=F32)
                scores.append(s + bias_ref[variant, h, :, j * Q_BLOCK:(j + 1) * Q_BLOCK])
            m = functools.reduce(jnp.maximum, [jnp.max(s, axis=-1, keepdims=True) for s in scores])
            p = jnp.concatenate([jnp.exp2(s - m).astype(BF16) for s in scores], axis=1)
            o = jnp.dot(p, vext_ref[h, qb * Q_BLOCK:qb * Q_BLOCK + win, :],
                        preferred_element_type=F32)
            acc_ref[rows, sl] = o[:, :HEAD_DIM] / o[:, HEAD_DIM:]
        y = acc_ref[rows, :]
        o_ref[0, rows, :] = (y * _rms_scale(y) * og_ref[...]).astype(o_ref.dtype)


def _attention(q3, k3, v3, bias, out_g):
    b, s, aw = q3.shape
    h = bias.shape[1]
    tile = Q_PER_STEP * Q_BLOCK
    blk = (1, tile, aw)
    cur = pl.BlockSpec(blk, lambda bi, i: (bi, i, 0))
    prev = pl.BlockSpec(blk, lambda bi, i: (bi, jnp.maximum(i - 1, 0), 0))
    return pl.pallas_call(
        _attn_kernel,
        grid=(b, s // tile),
        in_specs=[cur, prev, cur, prev, cur, _resident(bias.shape), _resident((1, aw))],
        out_specs=pl.BlockSpec(blk, lambda bi, i: (bi, i, 0)),
        out_shape=jax.ShapeDtypeStruct((b, s, aw), BF16),
        scratch_shapes=[pltpu.VMEM((tile, aw), F32),
                        pltpu.VMEM((h, 2 * tile, 2 * HEAD_DIM), BF16)],
        compiler_params=pltpu.CompilerParams(
            dimension_semantics=("arbitrary", "arbitrary"),
            vmem_limit_bytes=48 * MIB),
        name="attention",
    )(q3, k3, k3, v3, v3, bias, out_g.reshape(1, aw))


def _out_proj_kernel(x_ref, yc_ref, ya_ref, wc_ref, wa_ref, o_ref):
    for r0 in range(0, x_ref.shape[0], ROW_TILE):
        rows = slice(r0, r0 + ROW_TILE)
        acc = jnp.dot(yc_ref[rows, :], wc_ref[...], preferred_element_type=F32)
        acc = acc + jnp.dot(ya_ref[rows, :], wa_ref[...], preferred_element_type=F32)
        o_ref[rows, :] = x_ref[rows, :] + acc


def _out_proj(x2, yc2, ya2, w_out):
    m, d = x2.shape
    cw = yc2.shape[1]
    aw = ya2.shape[1]
    tm = WIDE_TILE
    row_blk = lambda width: pl.BlockSpec((tm, width), lambda i: (i, 0))
    return pl.pallas_call(
        _out_proj_kernel,
        grid=(m // tm,),
        in_specs=[row_blk(d), row_blk(cw), row_blk(aw),
                  _resident((cw, d), (0, 0)), _resident((aw, d), (cw // aw, 0))],
        out_specs=row_blk(d),
        out_shape=jax.ShapeDtypeStruct((m, d), F32),
        compiler_params=pltpu.CompilerParams(
            dimension_semantics=("arbitrary",),
            vmem_limit_bytes=56 * MIB),
        name="out_proj",
    )(x2, yc2, ya2, w_out, w_out)


def _ffn_kernel(x_ref, g_ref, w1_ref, w2_ref, o_ref, h_ref):
    @pl.when(pl.program_id(1) == 0)
    def _():
        for r0 in range(0, x_ref.shape[0], SUB_ROWS):
            x = x_ref[r0:r0 + SUB_ROWS, :]
            h_ref[r0:r0 + SUB_ROWS, :] = (x * _rms_scale(x) * g_ref[...]).astype(BF16)
            o_ref[r0:r0 + SUB_ROWS, :] = x

    for c0 in range(0, w1_ref.shape[1], FFN_CHUNK):
        a = jnp.dot(h_ref[...], w1_ref[:, c0:c0 + FFN_CHUNK], preferred_element_type=F32)
        a = jnp.maximum(a, 0.0)
        a = (a * a).astype(BF16)
        o_ref[...] += jnp.dot(a, w2_ref[c0:c0 + FFN_CHUNK, :], preferred_element_type=F32)


def _ffn(x2, g, w1, w2):
    m, d = x2.shape
    ff = w1.shape[1]
    tm, tf = FFN_ROWS, FFN_COLS
    return pl.pallas_call(
        _ffn_kernel,
        grid=(m // tm, ff // tf),
        in_specs=[
            pl.BlockSpec((tm, d), lambda i, f: (i, 0)),
            _resident((1, d)),
            pl.BlockSpec((d, tf), lambda i, f: (0, f)),
            pl.BlockSpec((tf, d), lambda i, f: (f, 0)),
        ],
        out_specs=pl.BlockSpec((tm, d), lambda i, f: (i, 0)),
        out_shape=jax.ShapeDtypeStruct((m, d), F32),
        scratch_shapes=[pltpu.VMEM((tm, d), BF16)],
        compiler_params=pltpu.CompilerParams(
            dimension_semantics=("arbitrary", "arbitrary"),
            vmem_limit_bytes=56 * MIB),
        name="ffn",
    )(x2, g, w1, w2)


def kernel(x, ln1_g, w_in, w_dw, b_dw, conv_ln_g, conv_ln_b, q_norm_g, k_norm_g, rel_bias,
           out_norm_conv_g, out_norm_attn_g, w_out, ln2_g, w_ff1, w_ff2):
    b, s, d = x.shape
    depth = w_in.shape[0]
    cw = w_dw.shape[2]
    aw = out_norm_attn_g.shape[1]
    ff = w_ff1.shape[2]
    qcol = 2 * cw // aw
    assert cw == aw and w_in.shape[2] == 2 * cw + 3 * aw and aw % HEAD_DIM == 0
    assert s % (Q_PER_STEP * Q_BLOCK) == 0 and s % ROW_TILE == 0 and (b * s) % FFN_ROWS == 0
    assert (b * s) % WIDE_TILE == 0 and ff % FFN_COLS == 0 and rel_bias.shape[2] == 2 * MAX_REL_DIST + 1
    x2 = x.reshape(b * s, d)
    for l in range(depth):
        bias, w_vg = _rel_bias(rel_bias[l], (w_in[l], 2 * cw, 0))
        vg, h, _ = _vg_proj(x2, ln1_g[l].reshape(1, d), w_vg, [])
        yc, (wq, wk, wv, wo) = _conv_branch(
            vg, w_dw[l], b_dw[l], conv_ln_g[l], conv_ln_b[l], out_norm_conv_g[l], s,
            [(w_in[l], aw, qcol), (w_in[l], aw, qcol + 1), (w_in[l], aw, qcol + 2), (w_out[l], d, 0)])
        q, k, v, (w1, w2) = _qkv_proj(h, wq, wk, wv, q_norm_g[l], k_norm_g[l],
                                      [(w_ff1[l], ff, 0), (w_ff2[l], d, 0)])
        ya = _attention(q.reshape(b, s, aw), k.reshape(b, s, aw), v.reshape(b, s, aw),
                        bias, out_norm_attn_g[l])
        x2 = _out_proj(x2, yc, ya.reshape(b * s, aw), wo)
        x2 = _ffn(x2, ln2_g[l].reshape(1, d), w1, w2)
    return x2.reshape(b, s, d)
```

```python
import functools
import math

import jax
import jax.numpy as jnp
import numpy as np
from jax import lax
from jax.experimental import pallas as pl
from jax.experimental.pallas import tpu as pltpu

F32 = jnp.float32
BF16 = jnp.bfloat16

EPS = 1e-6
CHUNK = 64
LEFT_CHUNKS = 8
HEAD_DIM = 128
CONV_KERNEL = 31
MAX_REL_DIST = 128
LANES = 128
BF16_ROWS = 16
MASK_VALUE = -1e30
LOG2E = math.log2(math.e)

Q_BLOCK = 256
KV_BLOCKS = 3
Q_PER_STEP = 2
HALO = 32
ROW_TILE = 512
WIDE_TILE = 1024
SUB_ROWS = 256
CONV_ROWS = 128
LN_ROWS = 64
NORM_ROWS = 64
MXU_COLS = 512
FFN_ROWS = 512
FFN_COLS = 2048
FFN_CHUNK = 512
MIB = 1024 * 1024


def _rms_scale(x):
    return lax.rsqrt(jnp.mean(x * x, axis=-1, keepdims=True) + EPS)


def _resident(shape, index=None):
    index = index or (0,) * len(shape)
    return pl.BlockSpec(shape, lambda *_: index, pipeline_mode=pl.Buffered(1))


def _cast_specs(casts, steps):
    in_specs, out_specs, shapes, args = [], [], [], []
    for arr, width, col in casts:
        assert arr.shape[0] % steps == 0 and (col + 1) * width <= arr.shape[1]
        slab = arr.shape[0] // steps
        in_specs.append(pl.BlockSpec((slab, width), lambda i, col=col: (i, col)))
        out_specs.append(pl.BlockSpec((slab, width), lambda i: (i, 0)))
        shapes.append(jax.ShapeDtypeStruct((arr.shape[0], width), BF16))
        args.append(arr)
    return in_specs, out_specs, shapes, args


def _vg_proj_kernel(x_ref, g_ref, w_ref, *refs, n_cast):
    cast_in = refs[:n_cast]
    vg_ref, h_ref = refs[n_cast:n_cast + 2]
    cast_out = refs[n_cast + 2:]
    tm = x_ref.shape[0]
    n = w_ref.shape[1]

    def norm_rows(r0, nrows):
        x = x_ref[r0:r0 + nrows, :]
        h_ref[r0:r0 + nrows, :] = (x * _rms_scale(x) * g_ref[...]).astype(BF16)

    for r0 in range(0, tm, NORM_ROWS):
        norm_rows(r0, NORM_ROWS)
    for s in range(tm // SUB_ROWS):
        rows = slice(s * SUB_ROWS, (s + 1) * SUB_ROWS)
        for c0 in range(0, n, MXU_COLS):
            y = jnp.dot(h_ref[rows, :], w_ref[:, c0:c0 + MXU_COLS], preferred_element_type=F32)
            for g in range(MXU_COLS // LANES):
                vg_ref[c0 // LANES + g, rows, :] = y[:, g * LANES:(g + 1) * LANES].astype(vg_ref.dtype)
    for src, dst in zip(cast_in, cast_out):
        dst[...] = src[...].astype(dst.dtype)


def _vg_proj(x2, g, w_vg, casts):
    m, d = x2.shape
    n = w_vg.shape[1]
    steps = m // WIDE_TILE
    cast_in, cast_out, cast_shapes, cast_args = _cast_specs(casts, steps)
    outs = pl.pallas_call(
        functools.partial(_vg_proj_kernel, n_cast=len(casts)),
        grid=(steps,),
        in_specs=[pl.BlockSpec((WIDE_TILE, d), lambda i: (i, 0)),
                  _resident((1, d)), _resident(w_vg.shape)] + cast_in,
        out_specs=[pl.BlockSpec((n // LANES, WIDE_TILE, LANES), lambda i: (0, i, 0)),
                   pl.BlockSpec((WIDE_TILE, d), lambda i: (i, 0))] + cast_out,
        out_shape=[jax.ShapeDtypeStruct((n // LANES, m, LANES), BF16),
                   jax.ShapeDtypeStruct((m, d), BF16)] + cast_shapes,
        compiler_params=pltpu.CompilerParams(
            dimension_semantics=("arbitrary",),
            vmem_limit_bytes=56 * MIB),
        name="vg_proj",
    )(x2, g, w_vg, *cast_args)
    return outs[0], outs[1], outs[2:]


def _qkv_proj_kernel(h_ref, wq_ref, wk_ref, wv_ref, qg_ref, kg_ref, *refs, n_cast):
    cast_in = refs[:n_cast]
    q_ref, k_ref, v_ref = refs[n_cast:n_cast + 3]
    cast_out = refs[n_cast + 3:]
    tm = h_ref.shape[0]
    aw = q_ref.shape[1]

    def head_norm(y, gain, out_ref, rows):
        for hd in range(aw // HEAD_DIM):
            sl = slice(hd * HEAD_DIM, (hd + 1) * HEAD_DIM)
            blk = y[:, sl]
            out_ref[rows, sl] = (blk * _rms_scale(blk) * gain).astype(out_ref.dtype)

    qg = qg_ref[...] * (HEAD_DIM ** -0.5 * LOG2E)
    kg = kg_ref[...]
    for s in range(tm // SUB_ROWS):
        rows = slice(s * SUB_ROWS, (s + 1) * SUB_ROWS)
        h = h_ref[rows, :]
        head_norm(jnp.dot(h, wq_ref[...], preferred_element_type=F32), qg, q_ref, rows)
        head_norm(jnp.dot(h, wk_ref[...], preferred_element_type=F32), kg, k_ref, rows)
        v_ref[rows, :] = jnp.dot(h, wv_ref[...], preferred_element_type=F32).astype(v_ref.dtype)
    for src, dst in zip(cast_in, cast_out):
        dst[...] = src[...].astype(dst.dtype)


def _qkv_proj(h2, wq, wk, wv, q_g, k_g, casts):
    m, d = h2.shape
    aw = wq.shape[1]
    steps = m // ROW_TILE
    row_blk = lambda width: pl.BlockSpec((ROW_TILE, width), lambda i: (i, 0))
    cast_in, cast_out, cast_shapes, cast_args = _cast_specs(casts, steps)
    outs = pl.pallas_call(
        functools.partial(_qkv_proj_kernel, n_cast=len(casts)),
        grid=(steps,),
        in_specs=[row_blk(d), _resident(wq.shape), _resident(wk.shape), _resident(wv.shape),
                  _resident((1, HEAD_DIM)), _resident((1, HEAD_DIM))] + cast_in,
        out_specs=[row_blk(aw)] * 3 + cast_out,
        out_shape=[jax.ShapeDtypeStruct((m, aw), BF16)] * 3 + cast_shapes,
        compiler_params=pltpu.CompilerParams(
            dimension_semantics=("arbitrary",),
            vmem_limit_bytes=48 * MIB),
        name="qkv_proj",
    )(h2, wq, wk, wv, q_g.reshape(1, HEAD_DIM), k_g.reshape(1, HEAD_DIM), *cast_args)
    return outs[0], outs[1], outs[2], outs[3:]


def _conv_kernel(vg_ref, halo_ref, cw_ref, cb_ref, lng_ref, lnb_ref, og_ref, o_ref,
                 hb_ref, hs_ref, y_ref, *, tiles_per_seq):
    groups = y_ref.shape[0]
    ts = y_ref.shape[1]
    width = groups * LANES
    first_tile = pl.program_id(0) % tiles_per_seq == 0
    first = HALO - (CONV_KERNEL - 1)

    def group_body(c, carry):
        halo = halo_ref[c].astype(F32) * jax.nn.sigmoid(halo_ref[groups + c].astype(F32))
        hb_ref[0:HALO, :] = jnp.where(first_tile, 0.0, halo)
        hb_ref[HALO:HALO + ts, :] = vg_ref[c].astype(F32) * jax.nn.sigmoid(vg_ref[groups + c].astype(F32))
        hb_ref[HALO + ts:, :] = jnp.zeros((BF16_ROWS, LANES), F32)
        for p in range(BF16_ROWS):
            hs_ref[p, :, :] = hb_ref[p:p + HALO + ts, :].astype(BF16)
        for r in range(0, ts, CONV_ROWS):
            prods = []
            for t in range(CONV_KERNEL):
                a, p = divmod(first + t, BF16_ROWS)
                lo = r + a * BF16_ROWS
                prods.append(cw_ref[c, t].astype(F32) * hs_ref[p, lo:lo + CONV_ROWS, :].astype(F32))
            y_ref[c, r:r + CONV_ROWS, :] = functools.reduce(lambda u, v: u + v, prods) + cb_ref[c]
        return carry

    lax.fori_loop(0, groups, group_body, 0)

    for r in range(0, ts, LN_ROWS):
        rows = slice(r, r + LN_ROWS)
        y = y_ref[:, rows, :]
        mu = jnp.sum(jnp.sum(y, axis=0), axis=-1, keepdims=True) / width
        yc = y - mu[None]
        var = jnp.sum(jnp.sum(yc * yc, axis=0), axis=-1, keepdims=True) / width
        z = yc * lax.rsqrt(var + EPS)[None] * lng_ref[...] + lnb_ref[...]
        sw = z * jax.nn.sigmoid(z)
        ms = jnp.sum(jnp.sum(sw * sw, axis=0), axis=-1, keepdims=True) / width
        out = sw * lax.rsqrt(ms + EPS)[None] * og_ref[...]
        for c in range(groups):
            o_ref[rows, c * LANES:(c + 1) * LANES] = out[c].astype(o_ref.dtype)


def _conv_branch(vg3, w_dw, b_dw, ln_g, ln_b, out_g, seq_len):
    n_groups2, m, _ = vg3.shape
    cw = w_dw.shape[1]
    groups = cw // LANES
    ts = ROW_TILE
    w_g = w_dw.reshape(CONV_KERNEL, groups, LANES).transpose(1, 0, 2)
    w_g = jnp.broadcast_to(w_g[:, :, None, :], (groups, CONV_KERNEL, CONV_ROWS, LANES)).astype(BF16)
    as_groups = lambda v: v.reshape(groups, 1, LANES)
    per_halo = ts // HALO
    return pl.pallas_call(
        functools.partial(_conv_kernel, tiles_per_seq=seq_len // ts),
        grid=(m // ts,),
        in_specs=[
            pl.BlockSpec((n_groups2, ts, LANES), lambda i: (0, i, 0)),
            pl.BlockSpec((n_groups2, HALO, LANES), lambda i: (0, jnp.maximum(i * per_halo - 1, 0), 0)),
            _resident((groups, CONV_KERNEL, CONV_ROWS, LANES)),
            _resident((groups, 1, LANES)), _resident((groups, 1, LANES)),
            _resident((groups, 1, LANES)), _resident((groups, 1, LANES)),
        ],
        out_specs=pl.BlockSpec((ts, cw), lambda i: (i, 0)),
        out_shape=jax.ShapeDtypeStruct((m, cw), BF16),
        scratch_shapes=[pltpu.VMEM((HALO + ts + BF16_ROWS, LANES), F32),
                        pltpu.VMEM((BF16_ROWS, HALO + ts, LANES), BF16),
                        pltpu.VMEM((groups, ts, LANES), F32)],
        compiler_params=pltpu.CompilerParams(
            dimension_semantics=("arbitrary",),
            vmem_limit_bytes=32 * MIB),
        name="conv_branch",
    )(vg3, vg3, w_g, as_groups(b_dw), as_groups(ln_g), as_groups(ln_b), as_groups(out_g))


def _rel_bias_kernel(row_ref, mask_ref, w_ref, o_ref, w_bf_ref):
    width = row_ref.shape[-1]
    x = jnp.broadcast_to(row_ref[0] * LOG2E, (Q_BLOCK, width))
    x = pltpu.roll(x, 0, 1, stride=1, stride_axis=0)
    bias = x[:, Q_BLOCK:] + mask_ref[...]
    col = lax.broadcasted_iota(jnp.int32, bias.shape, 1)
    for t in range(KV_BLOCKS):
        o_ref[t, 0] = jnp.where(col < (KV_BLOCKS - 1 - t) * Q_BLOCK, MASK_VALUE, bias)
    w_bf_ref[...] = w_ref[...].astype(w_bf_ref.dtype)


def _band_mask():
    r = np.arange(Q_BLOCK)[:, None] // CHUNK
    c = np.arange(KV_BLOCKS * Q_BLOCK)[None, :] // CHUNK
    visible = (c >= r) & (c <= r + LEFT_CHUNKS)
    return np.where(visible, 0.0, MASK_VALUE).astype(np.float32)


def _rel_bias(rel_table, cast):
    h, n_rel = rel_table.shape
    kw = KV_BLOCKS * Q_BLOCK
    width = Q_BLOCK + kw
    left = kw - MAX_REL_DIST
    right = width - left - n_rel
    row = jnp.pad(rel_table[:, ::-1], ((0, 0), (left, right)), mode="edge")
    row = row.reshape(h, 1, width)
    cast_in, cast_out, cast_shapes, cast_args = _cast_specs([cast], h)
    return pl.pallas_call(
        _rel_bias_kernel,
        grid=(h,),
        in_specs=[pl.BlockSpec((1, 1, width), lambda i: (i, 0, 0)),
                  pl.BlockSpec((Q_BLOCK, kw), lambda i: (0, 0))] + cast_in,
        out_specs=[pl.BlockSpec((KV_BLOCKS, 1, Q_BLOCK, kw), lambda i: (0, i, 0, 0))] + cast_out,
        out_shape=[jax.ShapeDtypeStruct((KV_BLOCKS, h, Q_BLOCK, kw), F32)] + cast_shapes,
        compiler_params=pltpu.CompilerParams(dimension_semantics=("arbitrary",)),
        name="rel_bias",
    )(row, jnp.asarray(_band_mask()), *cast_args)


def _attn_kernel(q_ref, kp_ref, kc_ref, vp_ref, vc_ref, bias_ref, og_ref, o_ref,
                 acc_ref, vext_ref):
    i = pl.program_id(1)
    n_heads = acc_ref.shape[1] // HEAD_DIM
    nt = (((1,), (1,)), ((), ()))
    win = KV_BLOCKS * Q_BLOCK
    tile = Q_PER_STEP * Q_BLOCK

    @pl.when((pl.program_id(0) == 0) & (i == 0))
    def _():
        vext_ref[:, :, HEAD_DIM:] = jnp.ones(vext_ref.shape[:2] + (HEAD_DIM,), BF16)

    for h in range(n_heads):
        sl = slice(h * HEAD_DIM, (h + 1) * HEAD_DIM)
        vext_ref[h, :tile, :HEAD_DIM] = vp_ref[0, :, sl]
        vext_ref[h, tile:, :HEAD_DIM] = vc_ref[0, :, sl]

    def key_block(n, sl):
        ref = kp_ref if n < Q_PER_STEP else kc_ref
        r0 = (n % Q_PER_STEP) * Q_BLOCK
        return ref[0, r0:r0 + Q_BLOCK, sl]

    for qb in range(Q_PER_STEP):
        rows = slice(qb * Q_BLOCK, (qb + 1) * Q_BLOCK)
        variant = jnp.minimum(i * Q_PER_STEP + qb, KV_BLOCKS - 1)
        for h in range(n_heads):
            sl = slice(h * HEAD_DIM, (h + 1) * HEAD_DIM)
            q = q_ref[0, rows, sl]
            scores = []
            for j in range(KV_BLOCKS):
                s = lax.dot_general(q, key_block(qb + j, sl), nt, preferred_element_type=F32)
                scores.append(s + bias_ref[variant, h, :, j * Q_BLOCK:(j + 1) * Q_BLOCK])
            m = functools.reduce(jnp.maximum, [jnp.max(s, axis=-1, keepdims=True) for s in scores])
            p = jnp.concatenate([jnp.exp2(s - m).astype(BF16) for s in scores], axis=1)
            o = jnp.dot(p, vext_ref[h, qb * Q_BLOCK:qb * Q_BLOCK + win, :],
                        preferred_element_type=F32)
            acc_ref[rows, sl] = o[:, :HEAD_DIM] / o[:, HEAD_DIM:]
        y = acc_ref[rows, :]
        o_ref[0, rows, :] = (y * _rms_scale(y) * og_ref[...]).astype(o_ref.dtype)


def _attention(q3, k3, v3, bias, out_g):
    b, s, aw = q3.shape
    h = bias.shape[1]
    tile = Q_PER_STEP * Q_BLOCK
    blk = (1, tile, aw)
    cur = pl.BlockSpec(blk, lambda bi, i: (bi, i, 0))
    prev = pl.BlockSpec(blk, lambda bi, i: (bi, jnp.maximum(i - 1, 0), 0))
    return pl.pallas_call(
        _attn_kernel,
        grid=(b, s // tile),
        in_specs=[cur, prev, cur, prev, cur, _resident(bias.shape), _resident((1, aw))],
        out_specs=pl.BlockSpec(blk, lambda bi, i: (bi, i, 0)),
        out_shape=jax.ShapeDtypeStruct((b, s, aw), BF16),
        scratch_shapes=[pltpu.VMEM((tile, aw), F32),
                        pltpu.VMEM((h, 2 * tile, 2 * HEAD_DIM), BF16)],
        compiler_params=pltpu.CompilerParams(
            dimension_semantics=("arbitrary", "arbitrary"),
            vmem_limit_bytes=48 * MIB),
        name="attention",
    )(q3, k3, k3, v3, v3, bias, out_g.reshape(1, aw))


def _out_proj_kernel(x_ref, yc_ref, ya_ref, wc_ref, wa_ref, o_ref):
    for r0 in range(0, x_ref.shape[0], ROW_TILE):
        rows = slice(r0, r0 + ROW_TILE)
        acc = jnp.dot(yc_ref[rows, :], wc_ref[...], preferred_element_type=F32)
        acc = acc + jnp.dot(ya_ref[rows, :], wa_ref[...], preferred_element_type=F32)
        o_ref[rows, :] = x_ref[rows, :] + acc


def _out_proj(x2, yc2, ya2, w_out):
    m, d = x2.shape
    cw = yc2.shape[1]
    aw = ya2.shape[1]
    tm = WIDE_TILE
    row_blk = lambda width: pl.BlockSpec((tm, width), lambda i: (i, 0))
    return pl.pallas_call(
        _out_proj_kernel,
        grid=(m // tm,),
        in_specs=[row_blk(d), row_blk(cw), row_blk(aw),
                  _resident((cw, d), (0, 0)), _resident((aw, d), (cw // aw, 0))],
        out_specs=row_blk(d),
        out_shape=jax.ShapeDtypeStruct((m, d), F32),
        compiler_params=pltpu.CompilerParams(
            dimension_semantics=("arbitrary",),
            vmem_limit_bytes=56 * MIB),
        name="out_proj",
    )(x2, yc2, ya2, w_out, w_out)


def _ffn_kernel(x_ref, g_ref, w1_ref, w2_ref, o_ref, h_ref):
    @pl.when(pl.program_id(1) == 0)
    def _():
        for r0 in range(0, x_ref.shape[0], SUB_ROWS):
            x = x_ref[r0:r0 + SUB_ROWS, :]
            h_ref[r0:r0 + SUB_ROWS, :] = (x * _rms_scale(x) * g_ref[...]).astype(BF16)
            o_ref[r0:r0 + SUB_ROWS, :] = x

    for c0 in range(0, w1_ref.shape[1], FFN_CHUNK):
        a = jnp.dot(h_ref[...], w1_ref[:, c0:c0 + FFN_CHUNK], preferred_element_type=F32)
        a = jnp.maximum(a, 0.0)
        a = (a * a).astype(BF16)
        o_ref[...] += jnp.dot(a, w2_ref[c0:c0 + FFN_CHUNK, :], preferred_element_type=F32)


def _ffn(x2, g, w1, w2):
    m, d = x2.shape
    ff = w1.shape[1]
    tm, tf = FFN_ROWS, FFN_COLS
    return pl.pallas_call(
        _ffn_kernel,
        grid=(m // tm, ff // tf),
        in_specs=[
            pl.BlockSpec((tm, d), lambda i, f: (i, 0)),
            _resident((1, d)),
            pl.BlockSpec((d, tf), lambda i, f: (0, f)),
            pl.BlockSpec((tf, d), lambda i, f: (f, 0)),
        ],
        out_specs=pl.BlockSpec((tm, d), lambda i, f: (i, 0)),
        out_shape=jax.ShapeDtypeStruct((m, d), F32),
        scratch_shapes=[pltpu.VMEM((tm, d), BF16)],
        compiler_params=pltpu.CompilerParams(
            dimension_semantics=("arbitrary", "arbitrary"),
            vmem_limit_bytes=56 * MIB),
        name="ffn",
    )(x2, g, w1, w2)


def kernel(x, ln1_g, w_in, w_dw, b_dw, conv_ln_g, conv_ln_b, q_norm_g, k_norm_g, rel_bias,
           out_norm_conv_g, out_norm_attn_g, w_out, ln2_g, w_ff1, w_ff2):
    b, s, d = x.shape
    depth = w_in.shape[0]
    cw = w_dw.shape[2]
    aw = out_norm_attn_g.shape[1]
    ff = w_ff1.shape[2]
    qcol = 2 * cw // aw
    assert cw == aw and w_in.shape[2] == 2 * cw + 3 * aw and aw % HEAD_DIM == 0
    assert s % (Q_PER_STEP * Q_BLOCK) == 0 and s % ROW_TILE == 0 and (b * s) % FFN_ROWS == 0
    assert (b * s) % WIDE_TILE == 0 and ff % FFN_COLS == 0 and rel_bias.shape[2] == 2 * MAX_REL_DIST + 1
    x2 = x.reshape(b * s, d)
    for l in range(depth):
        bias, w_vg = _rel_bias(rel_bias[l], (w_in[l], 2 * cw, 0))
        vg, h, (wq, wk, wv, wo) = _vg_proj(
            x2, ln1_g[l].reshape(1, d), w_vg,
            [(w_in[l], aw, qcol), (w_in[l], aw, qcol + 1), (w_in[l], aw, qcol + 2), (w_out[l], d, 0)])
        q, k, v, (w1, w2) = _qkv_proj(h, wq, wk, wv, q_norm_g[l], k_norm_g[l],
                                      [(w_ff1[l], ff, 0), (w_ff2[l], d, 0)])
        yc = _conv_branch(vg, w_dw[l], b_dw[l], conv_ln_g[l], conv_ln_b[l], out_norm_conv_g[l], s)
        ya = _attention(q.reshape(b, s, aw), k.reshape(b, s, aw), v.reshape(b, s, aw),
                        bias, out_norm_attn_g[l])
        x2 = _out_proj(x2, yc, ya.reshape(b * s, aw), wo)
        x2 = _ffn(x2, ln2_g[l].reshape(1, d), w1, w2)
    return x2.reshape(b, s, d)
```

```python
import functools
import math

import jax
import jax.numpy as jnp
import numpy as np
from jax import lax
from jax.experimental import pallas as pl
from jax.experimental.pallas import tpu as pltpu

F32 = jnp.float32
BF16 = jnp.bfloat16

EPS = 1e-6
CHUNK = 64
LEFT_CHUNKS = 8
HEAD_DIM = 128
CONV_KERNEL = 31
MAX_REL_DIST = 128
LANES = 128
BF16_ROWS = 16
MASK_VALUE = -1e30
LOG2E = math.log2(math.e)

Q_BLOCK = 256
KV_BLOCKS = 3
Q_PER_STEP = 2
HALO = 32
ROW_TILE = 512
WIDE_TILE = 512
STREAM_BUFFERS = 3
SUB_ROWS = 256
CONV_ROWS = 64
LN_ROWS = 64
NORM_ROWS = 64
MXU_COLS = 512
FFN_ROWS = 512
FFN_COLS = 2048
FFN_CHUNK = 512
MIB = 1024 * 1024


def _rms_scale(x):
    return lax.rsqrt(jnp.mean(x * x, axis=-1, keepdims=True) + EPS)


def _resident(shape, index=None):
    index = index or (0,) * len(shape)
    return pl.BlockSpec(shape, lambda *_: index, pipeline_mode=pl.Buffered(1))


def _cast_specs(casts, steps):
    in_specs, out_specs, shapes, args = [], [], [], []
    for arr, width, col in casts:
        assert arr.shape[0] % steps == 0 and (col + 1) * width <= arr.shape[1]
        slab = arr.shape[0] // steps
        in_specs.append(pl.BlockSpec((slab, width), lambda i, col=col: (i, col)))
        out_specs.append(pl.BlockSpec((slab, width), lambda i: (i, 0)))
        shapes.append(jax.ShapeDtypeStruct((arr.shape[0], width), BF16))
        args.append(arr)
    return in_specs, out_specs, shapes, args


def _vg_proj_kernel(x_ref, g_ref, w_ref, *refs, n_cast):
    cast_in = refs[:n_cast]
    vg_ref, h_ref = refs[n_cast:n_cast + 2]
    cast_out = refs[n_cast + 2:]
    tm = x_ref.shape[0]
    n = w_ref.shape[1]

    def norm_rows(r0, nrows):
        x = x_ref[r0:r0 + nrows, :]
        h_ref[r0:r0 + nrows, :] = (x * _rms_scale(x) * g_ref[...]).astype(BF16)

    for r0 in range(0, tm, NORM_ROWS):
        norm_rows(r0, NORM_ROWS)
    for s in range(tm // SUB_ROWS):
        rows = slice(s * SUB_ROWS, (s + 1) * SUB_ROWS)
        for c0 in range(0, n, MXU_COLS):
            y = jnp.dot(h_ref[rows, :], w_ref[:, c0:c0 + MXU_COLS], preferred_element_type=F32)
            for g in range(MXU_COLS // LANES):
                vg_ref[c0 // LANES + g, rows, :] = y[:, g * LANES:(g + 1) * LANES].astype(vg_ref.dtype)
    for src, dst in zip(cast_in, cast_out):
        dst[...] = src[...].astype(dst.dtype)


def _vg_proj(x2, g, w_vg, casts):
    m, d = x2.shape
    n = w_vg.shape[1]
    steps = m // WIDE_TILE
    cast_in, cast_out, cast_shapes, cast_args = _cast_specs(casts, steps)
    outs = pl.pallas_call(
        functools.partial(_vg_proj_kernel, n_cast=len(casts)),
        grid=(steps,),
        in_specs=[pl.BlockSpec((WIDE_TILE, d), lambda i: (i, 0)),
                  _resident((1, d)), _resident(w_vg.shape)] + cast_in,
        out_specs=[pl.BlockSpec((n // LANES, WIDE_TILE, LANES), lambda i: (0, i, 0)),
                   pl.BlockSpec((WIDE_TILE, d), lambda i: (i, 0))] + cast_out,
        out_shape=[jax.ShapeDtypeStruct((n // LANES, m, LANES), BF16),
                   jax.ShapeDtypeStruct((m, d), BF16)] + cast_shapes,
        compiler_params=pltpu.CompilerParams(
            dimension_semantics=("arbitrary",),
            vmem_limit_bytes=56 * MIB),
        name="vg_proj",
    )(x2, g, w_vg, *cast_args)
    return outs[0], outs[1], outs[2:]


def _qkv_proj_kernel(h_ref, wq_ref, wk_ref, wv_ref, qg_ref, kg_ref, *refs, n_cast):
    cast_in = refs[:n_cast]
    q_ref, k_ref, v_ref = refs[n_cast:n_cast + 3]
    cast_out = refs[n_cast + 3:]
    tm = h_ref.shape[0]
    aw = q_ref.shape[1]

    def head_norm(y, gain, out_ref, rows):
        for hd in range(aw // HEAD_DIM):
            sl = slice(hd * HEAD_DIM, (hd + 1) * HEAD_DIM)
            blk = y[:, sl]
            out_ref[rows, sl] = (blk * _rms_scale(blk) * gain).astype(out_ref.dtype)

    qg = qg_ref[...] * (HEAD_DIM ** -0.5 * LOG2E)
    kg = kg_ref[...]
    for s in range(tm // SUB_ROWS):
        rows = slice(s * SUB_ROWS, (s + 1) * SUB_ROWS)
        h = h_ref[rows, :]
        head_norm(jnp.dot(h, wq_ref[...], preferred_element_type=F32), qg, q_ref, rows)
        head_norm(jnp.dot(h, wk_ref[...], preferred_element_type=F32), kg, k_ref, rows)
        v_ref[rows, :] = jnp.dot(h, wv_ref[...], preferred_element_type=F32).astype(v_ref.dtype)
    for src, dst in zip(cast_in, cast_out):
        dst[...] = src[...].astype(dst.dtype)


def _qkv_proj(h2, wq, wk, wv, q_g, k_g, casts):
    m, d = h2.shape
    aw = wq.shape[1]
    steps = m // ROW_TILE
    row_blk = lambda width: pl.BlockSpec((ROW_TILE, width), lambda i: (i, 0))
    cast_in, cast_out, cast_shapes, cast_args = _cast_specs(casts, steps)
    outs = pl.pallas_call(
        functools.partial(_qkv_proj_kernel, n_cast=len(casts)),
        grid=(steps,),
        in_specs=[row_blk(d), _resident(wq.shape), _resident(wk.shape), _resident(wv.shape),
                  _resident((1, HEAD_DIM)), _resident((1, HEAD_DIM))] + cast_in,
        out_specs=[row_blk(aw)] * 3 + cast_out,
        out_shape=[jax.ShapeDtypeStruct((m, aw), BF16)] * 3 + cast_shapes,
        compiler_params=pltpu.CompilerParams(
            dimension_semantics=("arbitrary",),
            vmem_limit_bytes=48 * MIB),
        name="qkv_proj",
    )(h2, wq, wk, wv, q_g.reshape(1, HEAD_DIM), k_g.reshape(1, HEAD_DIM), *cast_args)
    return outs[0], outs[1], outs[2], outs[3:]


def _conv_kernel(vg_ref, halo_ref, cw_ref, cb_ref, lng_ref, lnb_ref, og_ref, o_ref,
                 hb_ref, hs_ref, y_ref, *, tiles_per_seq):
    groups = y_ref.shape[0]
    ts = y_ref.shape[1]
    width = groups * LANES
    first_tile = pl.program_id(0) % tiles_per_seq == 0
    first = HALO - (CONV_KERNEL - 1)

    def group_body(c, carry):
        halo = halo_ref[c].astype(F32) * jax.nn.sigmoid(halo_ref[groups + c].astype(F32))
        hb_ref[0:HALO, :] = jnp.where(first_tile, 0.0, halo)
        hb_ref[HALO:HALO + ts, :] = vg_ref[c].astype(F32) * jax.nn.sigmoid(vg_ref[groups + c].astype(F32))
        hb_ref[HALO + ts:, :] = jnp.zeros((BF16_ROWS, LANES), F32)
        for p in range(BF16_ROWS):
            hs_ref[p, :, :] = hb_ref[p:p + HALO + ts, :].astype(BF16)
        for r in range(0, ts, CONV_ROWS):
            prods = []
            for t in range(CONV_KERNEL):
                a, p = divmod(first + t, BF16_ROWS)
                lo = r + a * BF16_ROWS
                prods.append(cw_ref[c, t].astype(F32) * hs_ref[p, lo:lo + CONV_ROWS, :].astype(F32))
            y_ref[c, r:r + CONV_ROWS, :] = functools.reduce(lambda u, v: u + v, prods) + cb_ref[c]
        return carry

    lax.fori_loop(0, groups, group_body, 0)

    for r in range(0, ts, LN_ROWS):
        rows = slice(r, r + LN_ROWS)
        y = y_ref[:, rows, :]
        mu = jnp.sum(jnp.sum(y, axis=0), axis=-1, keepdims=True) / width
        yc = y - mu[None]
        var = jnp.sum(jnp.sum(yc * yc, axis=0), axis=-1, keepdims=True) / width
        z = yc * lax.rsqrt(var + EPS)[None] * lng_ref[...] + lnb_ref[...]
        sw = z * jax.nn.sigmoid(z)
        ms = jnp.sum(jnp.sum(sw * sw, axis=0), axis=-1, keepdims=True) / width
        out = sw * lax.rsqrt(ms + EPS)[None] * og_ref[...]
        for c in range(groups):
            o_ref[rows, c * LANES:(c + 1) * LANES] = out[c].astype(o_ref.dtype)


def _conv_branch(vg3, w_dw, b_dw, ln_g, ln_b, out_g, seq_len):
    n_groups2, m, _ = vg3.shape
    cw = w_dw.shape[1]
    groups = cw // LANES
    ts = ROW_TILE
    w_g = w_dw.reshape(CONV_KERNEL, groups, LANES).transpose(1, 0, 2)
    w_g = jnp.broadcast_to(w_g[:, :, None, :], (groups, CONV_KERNEL, CONV_ROWS, LANES)).astype(BF16)
    as_groups = lambda v: v.reshape(groups, 1, LANES)
    per_halo = ts // HALO
    return pl.pallas_call(
        functools.partial(_conv_kernel, tiles_per_seq=seq_len // ts),
        grid=(m // ts,),
        in_specs=[
            pl.BlockSpec((n_groups2, ts, LANES), lambda i: (0, i, 0)),
            pl.BlockSpec((n_groups2, HALO, LANES), lambda i: (0, jnp.maximum(i * per_halo - 1, 0), 0)),
            _resident((groups, CONV_KERNEL, CONV_ROWS, LANES)),
            _resident((groups, 1, LANES)), _resident((groups, 1, LANES)),
            _resident((groups, 1, LANES)), _resident((groups, 1, LANES)),
        ],
        out_specs=pl.BlockSpec((ts, cw), lambda i: (i, 0)),
        out_shape=jax.ShapeDtypeStruct((m, cw), BF16),
        scratch_shapes=[pltpu.VMEM((HALO + ts + BF16_ROWS, LANES), F32),
                        pltpu.VMEM((BF16_ROWS, HALO + ts, LANES), BF16),
                        pltpu.VMEM((groups, ts, LANES), F32)],
        compiler_params=pltpu.CompilerParams(
            dimension_semantics=("arbitrary",),
            vmem_limit_bytes=32 * MIB),
        name="conv_branch",
    )(vg3, vg3, w_g, as_groups(b_dw), as_groups(ln_g), as_groups(ln_b), as_groups(out_g))


def _rel_bias_kernel(row_ref, mask_ref, w_ref, o_ref, w_bf_ref):
    width = row_ref.shape[-1]
    x = jnp.broadcast_to(row_ref[0] * LOG2E, (Q_BLOCK, width))
    x = pltpu.roll(x, 0, 1, stride=1, stride_axis=0)
    bias = x[:, Q_BLOCK:] + mask_ref[...]
    col = lax.broadcasted_iota(jnp.int32, bias.shape, 1)
    for t in range(KV_BLOCKS):
        o_ref[t, 0] = jnp.where(col < (KV_BLOCKS - 1 - t) * Q_BLOCK, MASK_VALUE, bias)
    w_bf_ref[...] = w_ref[...].astype(w_bf_ref.dtype)


def _band_mask():
    r = np.arange(Q_BLOCK)[:, None] // CHUNK
    c = np.arange(KV_BLOCKS * Q_BLOCK)[None, :] // CHUNK
    visible = (c >= r) & (c <= r + LEFT_CHUNKS)
    return np.where(visible, 0.0, MASK_VALUE).astype(np.float32)


def _rel_bias(rel_table, cast):
    h, n_rel = rel_table.shape
    kw = KV_BLOCKS * Q_BLOCK
    width = Q_BLOCK + kw
    left = kw - MAX_REL_DIST
    right = width - left - n_rel
    row = jnp.pad(rel_table[:, ::-1], ((0, 0), (left, right)), mode="edge")
    row = row.reshape(h, 1, width)
    cast_in, cast_out, cast_shapes, cast_args = _cast_specs([cast], h)
    return pl.pallas_call(
        _rel_bias_kernel,
        grid=(h,),
        in_specs=[pl.BlockSpec((1, 1, width), lambda i: (i, 0, 0)),
                  pl.BlockSpec((Q_BLOCK, kw), lambda i: (0, 0))] + cast_in,
        out_specs=[pl.BlockSpec((KV_BLOCKS, 1, Q_BLOCK, kw), lambda i: (0, i, 0, 0))] + cast_out,
        out_shape=[jax.ShapeDtypeStruct((KV_BLOCKS, h, Q_BLOCK, kw), F32)] + cast_shapes,
        compiler_params=pltpu.CompilerParams(dimension_semantics=("arbitrary",)),
        name="rel_bias",
    )(row, jnp.asarray(_band_mask()), *cast_args)


def _attn_kernel(q_ref, kp_ref, kc_ref, vp_ref, vc_ref, bias_ref, og_ref, o_ref,
                 acc_ref, vext_ref):
    i = pl.program_id(1)
    n_heads = acc_ref.shape[1] // HEAD_DIM
    nt = (((1,), (1,)), ((), ()))
    win = KV_BLOCKS * Q_BLOCK
    tile = Q_PER_STEP * Q_BLOCK

    @pl.when((pl.program_id(0) == 0) & (i == 0))
    def _():
        vext_ref[:, :, HEAD_DIM:] = jnp.ones(vext_ref.shape[:2] + (HEAD_DIM,), BF16)

    for h in range(n_heads):
        sl = slice(h * HEAD_DIM, (h + 1) * HEAD_DIM)
        vext_ref[h, :tile, :HEAD_DIM] = vp_ref[0, :, sl]
        vext_ref[h, tile:, :HEAD_DIM] = vc_ref[0, :, sl]

    def key_block(n, sl):
        ref = kp_ref if n < Q_PER_STEP else kc_ref
        r0 = (n % Q_PER_STEP) * Q_BLOCK
        return ref[0, r0:r0 + Q_BLOCK, sl]

    for qb in range(Q_PER_STEP):
        rows = slice(qb * Q_BLOCK, (qb + 1) * Q_BLOCK)
        variant = jnp.minimum(i * Q_PER_STEP + qb, KV_BLOCKS - 1)
        for h in range(n_heads):
            sl = slice(h * HEAD_DIM, (h + 1) * HEAD_DIM)
            q = q_ref[0, rows, sl]
            scores = []
            for j in range(KV_BLOCKS):
                s = lax.dot_general(q, key_block(qb + j, sl), nt, preferred_element_type=F32)
                scores.append(s + bias_ref[variant, h, :, j * Q_BLOCK:(j + 1) * Q_BLOCK])
            m = functools.reduce(jnp.maximum, [jnp.max(s, axis=-1, keepdims=True) for s in scores])
            p = jnp.concatenate([jnp.exp2(s - m).astype(BF16) for s in scores], axis=1)
            o = jnp.dot(p, vext_ref[h, qb * Q_BLOCK:qb * Q_BLOCK + win, :],
                        preferred_element_type=F32)
            acc_ref[rows, sl] = o[:, :HEAD_DIM] / o[:, HEAD_DIM:]
        y = acc_ref[rows, :]
        o_ref[0, rows, :] = (y * _rms_scale(y) * og_ref[...]).astype(o_ref.dtype)


def _attention(q3, k3, v3, bias, out_g):
    b, s, aw = q3.shape
    h = bias.shape[1]
    tile = Q_PER_STEP * Q_BLOCK
    blk = (1, tile, aw)
    cur = pl.BlockSpec(blk, lambda bi, i: (bi, i, 0))
    prev = pl.BlockSpec(blk, lambda bi, i: (bi, jnp.maximum(i - 1, 0), 0))
    return pl.pallas_call(
        _attn_kernel,
        grid=(b, s // tile),
        in_specs=[cur, prev, cur, prev, cur, _resident(bias.shape), _resident((1, aw))],
        out_specs=pl.BlockSpec(blk, lambda bi, i: (bi, i, 0)),
        out_shape=jax.ShapeDtypeStruct((b, s, aw), BF16),
        scratch_shapes=[pltpu.VMEM((tile, aw), F32),
                        pltpu.VMEM((h, 2 * tile, 2 * HEAD_DIM), BF16)],
        compiler_params=pltpu.CompilerParams(
            dimension_semantics=("arbitrary", "arbitrary"),
            vmem_limit_bytes=48 * MIB),
        name="attention",
    )(q3, k3, k3, v3, v3, bias, out_g.reshape(1, aw))


def _out_proj_kernel(x_hbm, yc_hbm, ya_hbm, w_ref, o_hbm):
    m, d = x_hbm.shape
    cw = yc_hbm.shape[1]
    aw = ya_hbm.shape[1]
    tm = WIDE_TILE

    def body(x_ref, yc_ref, ya_ref, o_ref):
        acc = jnp.dot(yc_ref[...], w_ref[:cw, :], preferred_element_type=F32)
        acc = acc + jnp.dot(ya_ref[...], w_ref[cw:, :], preferred_element_type=F32)
        o_ref[...] = x_ref[...] + acc

    stream = lambda width: pl.BlockSpec((tm, width), lambda i: (i, 0),
                                        pipeline_mode=pl.Buffered(STREAM_BUFFERS))
    pltpu.emit_pipeline(
        body,
        grid=(m // tm,),
        in_specs=[stream(d), stream(cw), stream(aw)],
        out_specs=[pl.BlockSpec((tm, d), lambda i: (i, 0))],
    )(x_hbm, yc_hbm, ya_hbm, o_hbm)


def _out_proj(x2, yc2, ya2, w_out):
    m, d = x2.shape
    hbm = pl.BlockSpec(memory_space=pl.ANY)
    return pl.pallas_call(
        _out_proj_kernel,
        in_specs=[hbm, hbm, hbm, pl.BlockSpec(memory_space=pltpu.VMEM)],
        out_specs=hbm,
        out_shape=jax.ShapeDtypeStruct((m, d), F32),
        compiler_params=pltpu.CompilerParams(vmem_limit_bytes=56 * MIB),
        name="out_proj",
    )(x2, yc2, ya2, w_out)


def _ffn_kernel(x_ref, g_ref, w1_ref, w2_ref, o_ref, h_ref):
    @pl.when(pl.program_id(1) == 0)
    def _():
        for r0 in range(0, x_ref.shape[0], SUB_ROWS):
            x = x_ref[r0:r0 + SUB_ROWS, :]
            h_ref[r0:r0 + SUB_ROWS, :] = (x * _rms_scale(x) * g_ref[...]).astype(BF16)
            o_ref[r0:r0 + SUB_ROWS, :] = x

    for c0 in range(0, w1_ref.shape[1], FFN_CHUNK):
        a = jnp.dot(h_ref[...], w1_ref[:, c0:c0 + FFN_CHUNK], preferred_element_type=F32)
        a = jnp.maximum(a, 0.0)
        a = (a * a).astype(BF16)
        o_ref[...] += jnp.dot(a, w2_ref[c0:c0 + FFN_CHUNK, :], preferred_element_type=F32)


def _ffn(x2, g, w1, w2):
    m, d = x2.shape
    ff = w1.shape[1]
    tm, tf = FFN_ROWS, FFN_COLS
    return pl.pallas_call(
        _ffn_kernel,
        grid=(m // tm, ff // tf),
        in_specs=[
            pl.BlockSpec((tm, d), lambda i, f: (i, 0)),
            _resident((1, d)),
            pl.BlockSpec((d, tf), lambda i, f: (0, f)),
            pl.BlockSpec((tf, d), lambda i, f: (f, 0)),
        ],
        out_specs=pl.BlockSpec((tm, d), lambda i, f: (i, 0)),
        out_shape=jax.ShapeDtypeStruct((m, d), F32),
        scratch_shapes=[pltpu.VMEM((tm, d), BF16)],
        compiler_params=pltpu.CompilerParams(
            dimension_semantics=("arbitrary", "arbitrary"),
            vmem_limit_bytes=56 * MIB),
        name="ffn",
    )(x2, g, w1, w2)


def kernel(x, ln1_g, w_in, w_dw, b_dw, conv_ln_g, conv_ln_b, q_norm_g, k_norm_g, rel_bias,
           out_norm_conv_g, out_norm_attn_g, w_out, ln2_g, w_ff1, w_ff2):
    b, s, d = x.shape
    depth = w_in.shape[0]
    cw = w_dw.shape[2]
    aw = out_norm_attn_g.shape[1]
    ff = w_ff1.shape[2]
    qcol = 2 * cw // aw
    assert cw == aw and w_in.shape[2] == 2 * cw + 3 * aw and aw % HEAD_DIM == 0
    assert s % (Q_PER_STEP * Q_BLOCK) == 0 and s % ROW_TILE == 0 and (b * s) % FFN_ROWS == 0
    assert (b * s) % WIDE_TILE == 0 and ff % FFN_COLS == 0 and rel_bias.shape[2] == 2 * MAX_REL_DIST + 1
    x2 = x.reshape(b * s, d)
    for l in range(depth):
        bias, w_vg = _rel_bias(rel_bias[l], (w_in[l], 2 * cw, 0))
        vg, h, (wq, wk, wv, wo) = _vg_proj(
            x2, ln1_g[l].reshape(1, d), w_vg,
            [(w_in[l], aw, qcol), (w_in[l], aw, qcol + 1), (w_in[l], aw, qcol + 2), (w_out[l], d, 0)])
        q, k, v, (w1, w2) = _qkv_proj(h, wq, wk, wv, q_norm_g[l], k_norm_g[l],
                                      [(w_ff1[l], ff, 0), (w_ff2[l], d, 0)])
        yc = _conv_branch(vg, w_dw[l], b_dw[l], conv_ln_g[l], conv_ln_b[l], out_norm_conv_g[l], s)
        ya = _attention(q.reshape(b, s, aw), k.reshape(b, s, aw), v.reshape(b, s, aw),
                        bias, out_norm_attn_g[l])
        x2 = _out_proj(x2, yc, ya.reshape(b * s, aw), wo)
        x2 = _ffn(x2, ln2_g[l].reshape(1, d), w1, w2)
    return x2.reshape(b, s, d)
```
